```python
import jax, jax.numpy as jnp
from jax import lax
import numpy as np

D_MODEL = 2048
BATCH = 2
SEQ = 4096
DEPTH = 4

N_BRANCH = 4
BRANCH_WIDTH = D_MODEL // N_BRANCH
HEAD_DIM = 64
SB_HEADS = BRANCH_WIDTH // HEAD_DIM
MB_HEADS = BRANCH_WIDTH // HEAD_DIM
SB_WIDTH = SB_HEADS * HEAD_DIM
MB_WIDTH = MB_HEADS * HEAD_DIM
POOL_WINDOWS = (2, 4, 8, 16)
N_POOL_GROUPS = len(POOL_WINDOWS)
POOL_GROUP = BRANCH_WIDTH // N_POOL_GROUPS
POOL_WIDTH = N_POOL_GROUPS * POOL_GROUP
CONV_WIDTH = BRANCH_WIDTH
CONV_K = 3
IN_COLS = 3 * SB_WIDTH + POOL_WIDTH + 3 * MB_WIDTH + 3 * CONV_WIDTH + N_BRANCH * D_MODEL
Q_BLOCK = 128
MOBA_BLOCK = 256
MOBA_TOPK = 3
ROPE_THETA = 10000.0
N_EXPERTS = 64
TOP_K = 8
EXPERT_HIDDEN = 384
SHARED_HIDDEN = 384
ROUTED_SCALE = 2.5
GROUP_ROWS = 128
LN_EPS = 1e-5
ALPHA = (2 * DEPTH) ** 0.25
BETA = (8 * DEPTH) ** -0.25

kernel_name = 'hybrid_sb_pool_moba_conv_moe_trunk'


def _layer_norm(x, g, b):
    xf = x.astype(jnp.float32)
    mu = jnp.mean(xf, axis=-1, keepdims=True)
    var = jnp.mean(jnp.square(xf - mu), axis=-1, keepdims=True)
    y = (xf - mu) * lax.rsqrt(var + LN_EPS) * g.astype(jnp.float32) + b.astype(jnp.float32)
    return y.astype(x.dtype)


def _split_heads(t, n_heads):
    b, s, _ = t.shape
    return t.reshape(b, s, n_heads, HEAD_DIM).transpose(0, 2, 1, 3)


def _merge_heads(t):
    b, h, s, dh = t.shape
    return t.transpose(0, 2, 1, 3).reshape(b, s, h * dh)


def _rotary(t):
    s, dh = t.shape[2], t.shape[3]
    half = dh // 2
    inv_freq = ROPE_THETA ** (-jnp.arange(half, dtype=jnp.float32) / half)
    ang = jnp.arange(s, dtype=jnp.float32)[:, None] * inv_freq[None, :]
    cos, sin = jnp.cos(ang), jnp.sin(ang)
    tf = t.astype(jnp.float32)
    t1, t2 = tf[..., :half], tf[..., half:]
    return jnp.concatenate([t1 * cos - t2 * sin, t2 * cos + t1 * sin], axis=-1).astype(t.dtype)


def _stick_breaking_attention(q, k, v):
    b, h, s, dh = q.shape
    n_blocks = s // Q_BLOCK
    scale = dh ** -0.5
    k_pos = jnp.arange(s)

    def block(i):
        q0 = i * Q_BLOCK
        qb = lax.dynamic_slice_in_dim(q, q0, Q_BLOCK, axis=2)
        q_pos = q0 + jnp.arange(Q_BLOCK)
        z = jnp.einsum('bhqd,bhkd->bhqk', qb, k).astype(jnp.float32) * scale
        past = k_pos[None, :] < q_pos[:, None]
        log_keep = jnp.where(past, jax.nn.log_sigmoid(-z), 0.0)
        log_between = lax.cumsum(log_keep, axis=3, reverse=True) - log_keep
        w = jnp.where(past, jnp.exp(jax.nn.log_sigmoid(z) + log_between), 0.0)
        return jnp.einsum('bhqk,bhkd->bhqd', w.astype(v.dtype), v)

    out = lax.map(block, jnp.arange(n_blocks))
    return jnp.moveaxis(out, 0, 2).reshape(b, h, s, dh)


def _moba_attention(q, k, v):
    b, h, s, dh = q.shape
    n_kb = -(-s // MOBA_BLOCK)
    pad = n_kb * MOBA_BLOCK - s
    kb = jnp.pad(k, ((0, 0), (0, 0), (0, pad), (0, 0))).reshape(b, h, n_kb, MOBA_BLOCK, dh)
    vb = jnp.pad(v, ((0, 0), (0, 0), (0, pad), (0, 0))).reshape(b, h, n_kb, MOBA_BLOCK, dh)
    k_mean = jnp.mean(kb.astype(jnp.float32), axis=3)
    n_sel = min(MOBA_TOPK, n_kb - 1)
    n_qb = s // Q_BLOCK
    scale = dh ** -0.5
    b_idx = jnp.arange(b)[:, None, None, None]
    h_idx = jnp.arange(h)[None, :, None, None]

    def block(i):
        q0 = i * Q_BLOCK
        qb = lax.dynamic_slice_in_dim(q, q0, Q_BLOCK, axis=2)
        q_pos = q0 + jnp.arange(Q_BLOCK)
        own = q0 // MOBA_BLOCK
        k_own = lax.dynamic_index_in_dim(kb, own, axis=2, keepdims=False)
        v_own = lax.dynamic_index_in_dim(vb, own, axis=2, keepdims=False)
        k_pos = own * MOBA_BLOCK + jnp.arange(MOBA_BLOCK)
        s_own = jnp.einsum('bhqd,bhkd->bhqk', qb, k_own).astype(jnp.float32) * scale
        s_own = jnp.where(k_pos[None, :] <= q_pos[:, None], s_own, -jnp.inf)
        if n_sel == 0:
            p_own = jax.nn.softmax(s_own, axis=-1).astype(v.dtype)
            return jnp.einsum('bhqk,bhkd->bhqd', p_own, v_own)
        gate = jnp.einsum('bhqd,bhnd->bhqn', qb.astype(jnp.float32), k_mean)
        gate = jnp.where(jnp.arange(n_kb) < own, gate, -jnp.inf)
        _, sel = lax.top_k(gate, n_sel)
        valid = jnp.arange(n_sel) < own
        k_sel = kb[b_idx, h_idx, sel]
        v_sel = vb[b_idx, h_idx, sel]
        s_sel = jnp.einsum('bhqd,bhqnkd->bhqnk', qb, k_sel).astype(jnp.float32) * scale
        s_sel = jnp.where(valid[:, None], s_sel, -jnp.inf)
        n_flat = n_sel * MOBA_BLOCK
        logits = jnp.concatenate([s_sel.reshape(b, h, Q_BLOCK, n_flat), s_own], axis=-1)
        p = jax.nn.softmax(logits, axis=-1).astype(v.dtype)
        p_sel = p[..., :n_flat].reshape(b, h, Q_BLOCK, n_sel, MOBA_BLOCK)
        p_own = p[..., n_flat:]
        return (jnp.einsum('bhqnk,bhqnkd->bhqd', p_sel, v_sel)
                + jnp.einsum('bhqk,bhkd->bhqd', p_own, v_own))

    out = lax.map(block, jnp.arange(n_qb))
    return jnp.moveaxis(out, 0, 2).reshape(b, h, s, dh)


def _multiscale_pool(u, pool_w, pool_scale):
    b, s, _ = u.shape
    g = u.reshape(b, s, N_POOL_GROUPS, POOL_GROUP).astype(jnp.float32)
    cs = jnp.pad(jnp.cumsum(g, axis=1), ((0, 0), (1, 0), (0, 0), (0, 0)))
    win = jnp.array(POOL_WINDOWS, dtype=jnp.int32)
    t = jnp.arange(s, dtype=jnp.int32)[:, None]
    lo = jnp.maximum(t + 1 - win[None, :], 0)
    count = jnp.minimum(t + 1, win[None, :]).astype(jnp.float32)
    window_sum = cs[:, 1:] - cs[:, lo, jnp.arange(N_POOL_GROUPS)]
    mixed = (window_sum / count[None, :, :, None] - g).astype(u.dtype)
    y = jnp.einsum('bsgc,gcd->bsgd', mixed, pool_w)
    return y.reshape(b, s, POOL_WIDTH) * pool_scale


def _short_gated_conv(h, bg, cg, conv_w):
    z = cg * h
    y = lax.conv_general_dilated(z, conv_w[:, None, :].astype(z.dtype), window_strides=(1,),
                                 padding=[(CONV_K - 1, 0)], dimension_numbers=('NWC', 'WIO', 'NWC'),
                                 feature_group_count=z.shape[-1])
    return bg * y


def _hybrid_mixer(u, w_in, gate_b, pool_w, pool_scale, conv_w, w_branch, w_out):
    b, s, d = u.shape
    proj = u @ w_in
    o1 = 3 * SB_WIDTH
    o2 = o1 + POOL_WIDTH
    o3 = o2 + 3 * MB_WIDTH
    o4 = o3 + 3 * CONV_WIDTH
    sb_qkv, pool_in, mb_qkv, conv_in, gate_in = jnp.split(proj, [o1, o2, o3, o4], axis=-1)
    sq, sk, sv = jnp.split(sb_qkv, 3, axis=-1)
    y_sb = _merge_heads(_stick_breaking_attention(_split_heads(sq, SB_HEADS), _split_heads(sk, SB_HEADS),
                                                  _split_heads(sv, SB_HEADS)))
    y_pool = _multiscale_pool(pool_in, pool_w, pool_scale)
    mq, mk, mv = jnp.split(mb_qkv, 3, axis=-1)
    y_mb = _merge_heads(_moba_attention(_rotary(_split_heads(mq, MB_HEADS)), _rotary(_split_heads(mk, MB_HEADS)),
                                        _split_heads(mv, MB_HEADS)))
    ch, cb, cc = jnp.split(conv_in, 3, axis=-1)
    y_cv = _short_gated_conv(ch, cb, cc, conv_w)
    branches = jnp.stack([y_sb, y_pool, y_mb, y_cv], axis=2)
    lifted = jnp.einsum('bsnc,ncd->bsnd', branches, w_branch)
    gates = jax.nn.sigmoid(gate_in.reshape(b, s, N_BRANCH, d) + gate_b.reshape(N_BRANCH, d))
    merged = jnp.sum(gates * lifted, axis=2)
    return merged @ w_out


def _swiglu(t, wg, wu, wd):
    return (jax.nn.silu(t @ wg) * (t @ wu)) @ wd


def _routed_experts(t, idx, wts, w_gate, w_up, w_down):
    n, d = t.shape
    a = n * TOP_K
    flat_e = idx.reshape(-1)
    flat_tok = jnp.arange(a, dtype=jnp.int32) // TOP_K
    flat_w = wts.reshape(-1)
    order = jnp.argsort(flat_e)
    e_sorted = flat_e[order]
    counts = jnp.bincount(flat_e, length=N_EXPERTS)
    padded = (counts + GROUP_ROWS - 1) // GROUP_ROWS * GROUP_ROWS
    start_sorted = jnp.cumsum(counts) - counts
    end_padded = jnp.cumsum(padded)
    start_padded = end_padded - padded
    dest = start_padded[e_sorted] + (jnp.arange(a) - start_sorted[e_sorted])
    n_blocks = -(-(a + N_EXPERTS * (GROUP_ROWS - 1)) // GROUP_ROWS)
    rows = n_blocks * GROUP_ROWS
    row_tok = jnp.zeros((rows,), jnp.int32).at[dest].set(flat_tok[order])
    row_w = jnp.zeros((rows,), t.dtype).at[dest].set(flat_w[order].astype(t.dtype))
    block_start = jnp.arange(n_blocks) * GROUP_ROWS
    block_e = jnp.minimum(jnp.searchsorted(end_padded, block_start, side='right'), N_EXPERTS - 1)

    def block(args):
        tok, w, e = args
        xb = jnp.take(t, tok, axis=0)
        hb = jax.nn.silu(xb @ w_gate[e]) * (xb @ w_up[e])
        return (hb @ w_down[e]) * w[:, None]

    out = lax.map(block, (row_tok.reshape(n_blocks, GROUP_ROWS), row_w.reshape(n_blocks, GROUP_ROWS), block_e))
    return jax.ops.segment_sum(out.reshape(rows, d), row_tok, num_segments=n)


def _moe_ffn(u, router_w, router_bias, w_gate, w_up, w_down, sh_gate, sh_up, sh_down):
    b, s, d = u.shape
    t = u.reshape(b * s, d)
    affinity = jax.nn.sigmoid((t @ router_w).astype(jnp.float32))
    _, idx = lax.top_k(affinity + router_bias.astype(jnp.float32), TOP_K)
    sel = jnp.take_along_axis(affinity, idx, axis=-1)
    wts = sel / jnp.sum(sel, axis=-1, keepdims=True) * ROUTED_SCALE
    routed = _routed_experts(t, idx, wts.astype(t.dtype), w_gate, w_up, w_down)
    shared = _swiglu(t, sh_gate, sh_up, sh_down)
    return (routed + shared).reshape(b, s, d)


def setup_inputs(seed: int = 0) -> dict:
    key = jax.random.key(seed)
    ks = jax.random.split(key, 23)
    L, D, E, H, HS = DEPTH, D_MODEL, N_EXPERTS, EXPERT_HIDDEN, SHARED_HIDDEN

    def nrm(k, shape, scale):
        return jax.random.normal(k, shape, jnp.float32) * scale

    return {
        'x': nrm(ks[0], (BATCH, SEQ, D), 1.0),
        'c': nrm(ks[1], (BATCH, D), 1.0),
        'ada_w': nrm(ks[2], (L, D, 6 * D), 0.2 * D ** -0.5),
        'ada_b': nrm(ks[3], (L, 6 * D), 0.02),
        'w_in': nrm(ks[4], (L, D, IN_COLS), D ** -0.5),
        'gate_b': nrm(ks[5], (L, N_BRANCH * D), 0.02),
        'pool_w': nrm(ks[6], (L, N_POOL_GROUPS, POOL_GROUP, POOL_GROUP), POOL_GROUP ** -0.5),
        'pool_scale': 1.0 + nrm(ks[7], (L, POOL_WIDTH), 0.02),
        'conv_w': nrm(ks[8], (L, CONV_K, CONV_WIDTH), CONV_K ** -0.5),
        'w_branch': nrm(ks[9], (L, N_BRANCH, BRANCH_WIDTH, D), BRANCH_WIDTH ** -0.5),
        'w_out': nrm(ks[10], (L, D, D), BETA * D ** -0.5),
        'ln1_g': 1.0 + nrm(ks[11], (L, D), 0.02),
        'ln1_b': nrm(ks[12], (L, D), 0.02),
        'router_w': nrm(ks[13], (L, D, E), D ** -0.5),
        'router_bias': nrm(ks[14], (L, E), 0.01),
        'exp_w_gate': nrm(ks[15], (L, E, D, H), D ** -0.5),
        'exp_w_up': nrm(ks[16], (L, E, D, H), D ** -0.5),
        'exp_w_down': nrm(ks[17], (L, E, H, D), BETA * H ** -0.5),
        'sh_w_gate': nrm(ks[18], (L, D, HS), D ** -0.5),
        'sh_w_up': nrm(ks[19], (L, D, HS), D ** -0.5),
        'sh_w_down': nrm(ks[20], (L, HS, D), BETA * HS ** -0.5),
        'ln2_g': 1.0 + nrm(ks[21], (L, D), 0.02),
        'ln2_b': nrm(ks[22], (L, D), 0.02),
    }


def reference(x, c, ada_w, ada_b, w_in, gate_b, pool_w, pool_scale, conv_w, w_branch, w_out, ln1_g, ln1_b,
              router_w, router_bias, exp_w_gate, exp_w_up, exp_w_down, sh_w_gate, sh_w_up, sh_w_down, ln2_g, ln2_b):
    cond = jax.nn.silu(c)
    for l in range(DEPTH):
        mod = cond @ ada_w[l] + ada_b[l]
        shift1, scale1, gate1, shift2, scale2, gate2 = [m[:, None, :] for m in jnp.split(mod, 6, axis=-1)]
        u = x * (1 + scale1) + shift1
        mix = _hybrid_mixer(u, w_in[l], gate_b[l], pool_w[l], pool_scale[l], conv_w[l], w_branch[l], w_out[l])
        x = _layer_norm(ALPHA * x + (1 + gate1) * mix, ln1_g[l], ln1_b[l])
        u = x * (1 + scale2) + shift2
        ffn = _moe_ffn(u, router_w[l], router_bias[l], exp_w_gate[l], exp_w_up[l], exp_w_down[l],
                       sh_w_gate[l], sh_w_up[l], sh_w_down[l])
        x = _layer_norm(ALPHA * x + (1 + gate2) * ffn, ln2_g[l], ln2_b[l])
    return x
```

```python
import functools

import numpy as np
import jax
import jax.numpy as jnp
from jax import lax
from jax.experimental import pallas as pl
from jax.experimental.pallas import tpu as pltpu

F32 = jnp.float32
BF16 = jnp.bfloat16
I32 = jnp.int32

HEAD_DIM = 64
N_BRANCH = 4
BRANCH_WIDTH = 512
POOL_WINDOWS = (2, 4, 8, 16)
POOL_GROUP = 128
POOL_HALO = 16
CONV_K = 3
MOBA_BLOCK = 256
MOBA_TOPK = 3
ROPE_THETA = 10000.0
N_EXPERTS = 64
TOP_K = 8
EXPERT_HIDDEN = 384
ROUTED_SCALE = 2.5
LN_EPS = 1e-5
NEG_INF = float("-inf")

LANES = 128
VMEM_LIMIT = 56 * 1024 * 1024


def _cparams(n_axes, vmem=VMEM_LIMIT):
    return pltpu.CompilerParams(dimension_semantics=("arbitrary",) * n_axes, vmem_limit_bytes=vmem)


def _dot(a, b):
    return jnp.dot(a, b, preferred_element_type=F32)


def _dot_nt(a, b):
    return lax.dot_general(a, b, (((1,), (1,)), ((), ())), preferred_element_type=F32)


def _split_bf16(x):
    hi = x.astype(BF16)
    lo = (x - hi.astype(F32)).astype(BF16)
    return hi, lo


def _ada_kernel(c_ref, w_ref, b_ref, o_ref):
    c = c_ref[...]
    cond = c * jax.nn.sigmoid(c)
    o_ref[...] = jnp.dot(cond, w_ref[...], preferred_element_type=F32,
                         precision=lax.Precision.HIGHEST) + b_ref[...]


def _ada(c, ada_w, ada_b):
    depth, d, d6 = ada_w.shape
    b = c.shape[0]
    rows = 8
    tn = 1024
    c8 = jnp.zeros((rows, d), F32).at[:b].set(c)
    out = pl.pallas_call(
        _ada_kernel,
        out_shape=jax.ShapeDtypeStruct((depth, rows, d6), F32),
        grid=(depth, d6 // tn),
        in_specs=[pl.BlockSpec((rows, d), lambda l, n: (0, 0)),
                  pl.BlockSpec((None, d, tn), lambda l, n: (l, 0, n)),
                  pl.BlockSpec((None, 1, tn), lambda l, n: (l, 0, n))],
        out_specs=pl.BlockSpec((None, rows, tn), lambda l, n: (l, 0, n)),
        compiler_params=_cparams(2),
        name="ada_mod",
    )(c8, ada_w, ada_b.reshape(depth, 1, d6))
    return out[:, :b].reshape(depth, b, 6, 1, d)


def _mod_spec(which, d, tiles_per_seq):
    return pl.BlockSpec((None, None, 1, d), lambda i, *_: (i // tiles_per_seq, which, 0, 0))


def _modulate_kernel(x_ref, sh_ref, sc_ref, u_ref):
    u_ref[...] = (x_ref[...] * (1.0 + sc_ref[...]) + sh_ref[...]).astype(u_ref.dtype)


def _modulate(x2, mod_l, seq, tm=512):
    n, d = x2.shape
    tm = min(tm, seq)
    tps = seq // tm
    return pl.pallas_call(
        _modulate_kernel,
        out_shape=jax.ShapeDtypeStruct((n, d), BF16),
        grid=(n // tm,),
        in_specs=[pl.BlockSpec((tm, d), lambda i: (i, 0)), _mod_spec(0, d, tps), _mod_spec(1, d, tps)],
        out_specs=pl.BlockSpec((tm, d), lambda i: (i, 0)),
        compiler_params=_cparams(1),
        name="modulate",
    )(x2, mod_l, mod_l)


def _inproj_kernel(u_ref, w_ref, cos_ref, sin_ref, o_ref, wbf_ref, *, rot_tile, tn):
    n = pl.program_id(0)
    m = pl.program_id(1)

    @pl.when(m == 0)
    def _():
        wbf_ref[...] = w_ref[...].astype(BF16)

    acc = _dot(u_ref[...], wbf_ref[...])

    @pl.when(n == rot_tile)
    def _():
        reps = tn // LANES
        cos = jnp.concatenate([cos_ref[...]] * reps, axis=1)
        sin = jnp.concatenate([sin_ref[...]] * reps, axis=1)
        lane = lax.broadcasted_iota(I32, acc.shape, 1)
        first_half = (lane % HEAD_DIM) < (HEAD_DIM // 2)
        half = HEAD_DIM // 2
        partner = jnp.where(first_half, pltpu.roll(acc, tn - half, 1), pltpu.roll(acc, half, 1))
        o_ref[...] = (acc * cos + partner * sin).astype(o_ref.dtype)

    @pl.when(n != rot_tile)
    def _():
        o_ref[...] = acc.astype(o_ref.dtype)


def _inproj(u, w_in, layer, cos_t, sin_t, seq, rot_col, tm=1024, tn=1024):
    n, d = u.shape
    cols = w_in.shape[2]
    tm = min(tm, seq)
    tps = seq // tm
    assert rot_col % tn == 0
    kern = functools.partial(_inproj_kernel, rot_tile=rot_col // tn, tn=tn)
    return pl.pallas_call(
        kern,
        out_shape=jax.ShapeDtypeStruct((n, cols), BF16),
        grid=(cols // tn, n // tm),
        in_specs=[pl.BlockSpec((tm, d), lambda j, i: (i, 0)),
                  pl.BlockSpec((None, d, tn), lambda j, i: (layer, 0, j)),
                  pl.BlockSpec((tm, LANES), lambda j, i: (i % tps, 0)),
                  pl.BlockSpec((tm, LANES), lambda j, i: (i % tps, 0))],
        out_specs=pl.BlockSpec((tm, tn), lambda j, i: (i, j)),
        scratch_shapes=[pltpu.VMEM((d, tn), BF16)],
        compiler_params=_cparams(2),
        name="in_proj",
    )(u, w_in, cos_t, sin_t)


def _rope_tables(seq):
    half = HEAD_DIM // 2
    inv_freq = (ROPE_THETA ** (-np.arange(half, dtype=np.float32) / half)).astype(np.float32)
    ang = (np.arange(seq, dtype=np.float32)[:, None] * inv_freq[None, :]).astype(np.float32)
    cos, sin = np.cos(ang.astype(np.float64)), np.sin(ang.astype(np.float64))
    cos_h = np.concatenate([cos, cos], axis=1)
    sin_h = np.concatenate([-sin, sin], axis=1)
    reps = LANES // HEAD_DIM
    return (jnp.asarray(np.tile(cos_h, (1, reps)), F32), jnp.asarray(np.tile(sin_h, (1, reps)), F32))


def _sb_kernel(q_ref, k_ref, v_ref, o_ref, *, tq, tk):
    qi = pl.program_id(2)
    q0 = qi * tq
    r = lax.broadcasted_iota(I32, (tk, tk), 0)
    c = lax.broadcasted_iota(I32, (tk, tk), 1)
    later = (r > c).astype(BF16)
    heads = LANES // HEAD_DIM
    qs = [q_ref[:, h * HEAD_DIM:(h + 1) * HEAD_DIM] * (HEAD_DIM ** -0.5) for h in range(heads)]

    def block(j0, h, carry, acc, masked):
        k = k_ref[pl.ds(j0, tk), h * HEAD_DIM:(h + 1) * HEAD_DIM]
        v = v_ref[pl.ds(j0, tk), h * HEAD_DIM:(h + 1) * HEAD_DIM]
        z = _dot_nt(qs[h], k)
        softplus = jnp.maximum(z, 0.0) + jnp.log(1.0 + jnp.exp(-jnp.abs(z)))
        log_keep = -softplus
        if masked:
            qpos = q0 + lax.broadcasted_iota(I32, (tq, tk), 0)
            kpos = j0 + lax.broadcasted_iota(I32, (tq, tk), 1)
            past = kpos < qpos
            log_keep = jnp.where(past, log_keep, 0.0)
        hi, lo = _split_bf16(log_keep)
        between = _dot(hi, later) + _dot(lo, later) + carry
        w = jnp.exp((z - softplus) + between)
        if masked:
            w = jnp.where(past, w, 0.0)
        acc = acc + _dot(w.astype(BF16), v)
        carry = carry + jnp.sum(log_keep, axis=1, keepdims=True)
        return carry, acc

    state = [(jnp.zeros((tq, 1), F32), jnp.zeros((tq, HEAD_DIM), F32)) for _ in range(heads)]
    n_diag = tq // tk
    for d in range(n_diag - 1, -1, -1):
        j0 = pl.multiple_of(q0 + d * tk, tk)
        state = [block(j0, h, *state[h], True) for h in range(heads)]

    def body(jj, st):
        j0 = pl.multiple_of(q0 - (jj + 1) * tk, tk)
        new = [block(j0, h, st[2 * h], st[2 * h + 1], False) for h in range(heads)]
        return tuple(x for pair in new for x in pair)

    flat = lax.fori_loop(0, qi * n_diag, body, tuple(x for pair in state for x in pair))
    o_ref[...] = jnp.concatenate([flat[2 * h + 1] for h in range(heads)], axis=1).astype(o_ref.dtype)


def _sb_attention(proj3, q_blk, k_blk, v_blk, tq=256, tk=128):
    b, seq, _ = proj3.shape
    pairs = BRANCH_WIDTH // LANES
    kern = functools.partial(_sb_kernel, tq=tq, tk=tk)
    return pl.pallas_call(
        kern,
        out_shape=jax.ShapeDtypeStruct((b, seq, BRANCH_WIDTH), BF16),
        grid=(b, pairs, seq // tq),
        in_specs=[pl.BlockSpec((None, tq, LANES), lambda bi, p, qi: (bi, qi, q_blk + p)),
                  pl.BlockSpec((None, seq, LANES), lambda bi, p, qi: (bi, 0, k_blk + p)),
                  pl.BlockSpec((None, seq, LANES), lambda bi, p, qi: (bi, 0, v_blk + p))],
        out_specs=pl.BlockSpec((None, tq, LANES), lambda bi, p, qi: (bi, qi, p)),
        compiler_params=_cparams(3),
        name="sb_attention",
    )(proj3, proj3, proj3)


def _moba_kernel(q_ref, k_ref, v_ref, o_ref, kmean_ref, *, n_kb):
    own = pl.program_id(2)
    tq = MOBA_BLOCK
    heads = LANES // HEAD_DIM

    @pl.when(own == 0)
    def _():
        for jb in range(n_kb):
            kb = k_ref[jb * MOBA_BLOCK:(jb + 1) * MOBA_BLOCK, :].astype(F32)
            kmean_ref[jb:jb + 1, :] = jnp.mean(kb, axis=0, keepdims=True)

    lane = lax.broadcasted_iota(I32, (tq, n_kb), 1)
    own0 = pl.multiple_of(own * MOBA_BLOCK, MOBA_BLOCK)
    row = lax.broadcasted_iota(I32, (tq, MOBA_BLOCK), 0)
    col = lax.broadcasted_iota(I32, (tq, MOBA_BLOCK), 1)
    causal = col <= row

    qs, sel_bias, state = [], [], []
    for h in range(heads):
        hs = slice(h * HEAD_DIM, (h + 1) * HEAD_DIM)
        q = q_ref[:, hs] * (HEAD_DIM ** -0.5)
        qs.append(q)
        km_hi, km_lo = _split_bf16(kmean_ref[:, hs])
        gate = _dot_nt(q, km_hi) + _dot_nt(q, km_lo)
        beaten = jnp.zeros((tq, n_kb), I32)
        for jp in range(n_kb):
            g = gate[:, jp:jp + 1]
            beats = (g > gate) | ((g == gate) & (jp < lane))
            beaten = beaten + jnp.where(beats, 1, 0) * (jp < own).astype(I32)
        chosen = (beaten < MOBA_TOPK) & (lane < own)
        sel_bias.append(jnp.where(chosen, 0.0, NEG_INF))
        s = _dot_nt(q, k_ref[pl.ds(own0, MOBA_BLOCK), hs])
        s = jnp.where(causal, s, NEG_INF)
        m = jnp.max(s, axis=1, keepdims=True)
        p = jnp.exp(s - m)
        l = jnp.sum(p, axis=1, keepdims=True)
        acc = _dot(p.astype(BF16), v_ref[pl.ds(own0, MOBA_BLOCK), hs])
        state += [m, l, acc]

    def body(j, st):
        j0 = pl.multiple_of(j * MOBA_BLOCK, MOBA_BLOCK)
        out = []
        for h in range(heads):
            hs = slice(h * HEAD_DIM, (h + 1) * HEAD_DIM)
            m, l, acc = st[3 * h], st[3 * h + 1], st[3 * h + 2]
            bias = jnp.max(jnp.where(lane == j, sel_bias[h], NEG_INF), axis=1, keepdims=True)
            s = _dot_nt(qs[h], k_ref[pl.ds(j0, MOBA_BLOCK), hs]) + bias
            m_new = jnp.maximum(m, jnp.max(s, axis=1, keepdims=True))
            alpha = jnp.exp(m - m_new)
            p = jnp.exp(s - m_new)
            l = alpha * l + jnp.sum(p, axis=1, keepdims=True)
            acc = alpha * acc + _dot(p.astype(BF16), v_ref[pl.ds(j0, MOBA_BLOCK), hs])
            out += [m_new, l, acc]
        return tuple(out)

    st = lax.fori_loop(0, own, body, tuple(state))
    o_ref[...] = jnp.concatenate([st[3 * h + 2] / st[3 * h + 1] for h in range(heads)],
                                 axis=1).astype(o_ref.dtype)


def _moba_attention(proj3, q_blk, k_blk, v_blk):
    b, seq, _ = proj3.shape
    assert seq % MOBA_BLOCK == 0
    n_kb = seq // MOBA_BLOCK
    pairs = BRANCH_WIDTH // LANES
    kern = functools.partial(_moba_kernel, n_kb=n_kb)
    return pl.pallas_call(
        kern,
        out_shape=jax.ShapeDtypeStruct((b, seq, BRANCH_WIDTH), BF16),
        grid=(b, pairs, n_kb),
        in_specs=[pl.BlockSpec((None, MOBA_BLOCK, LANES), lambda bi, p, qi: (bi, qi, q_blk + p)),
                  pl.BlockSpec((None, seq, LANES), lambda bi, p, qi: (bi, 0, k_blk + p)),
                  pl.BlockSpec((None, seq, LANES), lambda bi, p, qi: (bi, 0, v_blk + p))],
        out_specs=pl.BlockSpec((None, MOBA_BLOCK, LANES), lambda bi, p, qi: (bi, qi, p)),
        scratch_shapes=[pltpu.VMEM((n_kb, LANES), F32)],
        compiler_params=_cparams(3),
        name="moba_attention",
    )(proj3, proj3, proj3)


def _local_kernel(p_ref, ph_ref, ch_ref, chh_ref, cb_ref, cc_ref, cch_ref, pw_ref, ps_ref, cw_ref,
                  yp_ref, yc_ref, *, tm, tiles_per_seq):
    i = pl.program_id(0)
    seq_tile = i % tiles_per_seq
    has_left = seq_tile > 0

    def with_halo(halo_ref, body_ref):
        halo = jnp.where(has_left, halo_ref[...].astype(F32), 0.0)
        return jnp.concatenate([halo, body_ref[...].astype(F32)], axis=0)

    x = with_halo(ph_ref, p_ref)
    pos = seq_tile * tm + lax.broadcasted_iota(I32, (tm, 1), 0)
    for g, win in enumerate(POOL_WINDOWS):
        gs = slice(g * POOL_GROUP, (g + 1) * POOL_GROUP)
        xg = x[:, gs]
        s, k = xg, 1
        while k < win:
            s = s[k:] + s[:-k]
            k *= 2
        start = POOL_HALO + 1 - win
        window_sum = s[start:start + tm]
        count = jnp.minimum(pos + 1, win).astype(F32)
        mixed = window_sum / count - xg[POOL_HALO:]
        y = _dot(mixed.astype(BF16), pw_ref[g].astype(BF16)) * ps_ref[:, gs]
        yp_ref[:, gs] = y.astype(yp_ref.dtype)

    z = with_halo(chh_ref, ch_ref) * with_halo(cch_ref, cc_ref)
    y = jnp.zeros((tm, z.shape[1]), F32)
    for k in range(CONV_K):
        off = POOL_HALO - (CONV_K - 1) + k
        y = y + cw_ref[k:k + 1, :] * z[off:off + tm]
    yc_ref[...] = (cb_ref[...].astype(F32) * y).astype(yc_ref.dtype)


def _local_mixers(proj, pool_w, pool_scale, conv_w, seq, pool_blk, ch_blk, cb_blk, cc_blk, tm=512):
    n = proj.shape[0]
    w = BRANCH_WIDTH
    tm = min(tm, seq)
    tps = seq // tm
    hpt = tm // POOL_HALO

    def body(blk):
        return pl.BlockSpec((tm, w), lambda i: (i, blk))

    def halo(blk):
        return pl.BlockSpec((POOL_HALO, w), lambda i: (jnp.maximum(i * hpt - 1, 0), blk))

    kern = functools.partial(_local_kernel, tm=tm, tiles_per_seq=tps)
    return pl.pallas_call(
        kern,
        out_shape=(jax.ShapeDtypeStruct((n, w), BF16), jax.ShapeDtypeStruct((n, w), BF16)),
        grid=(n // tm,),
        in_specs=[body(pool_blk), halo(pool_blk), body(ch_blk), halo(ch_blk), body(cb_blk),
                  body(cc_blk), halo(cc_blk),
                  pl.BlockSpec(pool_w.shape, lambda i: (0, 0, 0)),
                  pl.BlockSpec((1, w), lambda i: (0, 0)),
                  pl.BlockSpec((CONV_K, w), lambda i: (0, 0))],
        out_specs=(pl.BlockSpec((tm, w), lambda i: (i, 0)), pl.BlockSpec((tm, w), lambda i: (i, 0))),
        compiler_params=_cparams(1),
        name="pool_conv",
    )(proj, proj, proj, proj, proj, proj, proj, pool_w, pool_scale.reshape(1, w), conv_w)


def _merge_kernel(b0, b1, b2, b3, g0, g1, g2, g3, wb_ref, gb_ref, o_ref, wbf_ref):
    @pl.when(pl.program_id(1) == 0)
    def _():
        wbf_ref[...] = wb_ref[...].astype(BF16)

    merged = None
    for nb, (br, gr) in enumerate(((b0, g0), (b1, g1), (b2, g2), (b3, g3))):
        lifted = _dot(br[...], wbf_ref[nb])
        gate = jax.nn.sigmoid(gr[...].astype(F32) + gb_ref[nb])
        term = gate * lifted
        merged = term if merged is None else merged + term
    o_ref[...] = merged.astype(o_ref.dtype)


def _merge(branches, proj, w_branch, gate_b, layer, gate_col, tm=1024, tn=512):
    n = proj.shape[0]
    d = w_branch.shape[3]
    assert gate_col % tn == 0 and d % tn == 0
    g_specs = [pl.BlockSpec((tm, tn), functools.partial(lambda j, i, nb: (i, (gate_col + nb * d) // tn + j), nb=nb))
               for nb in range(N_BRANCH)]
    b_specs = [pl.BlockSpec((tm, BRANCH_WIDTH), lambda j, i: (i, 0))] * N_BRANCH
    return pl.pallas_call(
        _merge_kernel,
        out_shape=jax.ShapeDtypeStruct((n, d), BF16),
        grid=(d // tn, n // tm),
        in_specs=b_specs + g_specs + [
            pl.BlockSpec((None, N_BRANCH, BRANCH_WIDTH, tn), lambda j, i: (layer, 0, 0, j)),
            pl.BlockSpec((N_BRANCH, 1, tn), lambda j, i: (0, 0, j))],
        out_specs=pl.BlockSpec((tm, tn), lambda j, i: (i, j)),
        scratch_shapes=[pltpu.VMEM((N_BRANCH, BRANCH_WIDTH, tn), BF16)],
        compiler_params=_cparams(2),
        name="gated_merge",
    )(*branches, proj, proj, proj, proj, w_branch, gate_b.reshape(N_BRANCH, 1, d))


def _post_norm(x, gate, y, g, b, alpha):
    h = alpha * x + (1.0 + gate) * y
    mu = jnp.mean(h, axis=1, keepdims=True)
    hc = h - mu
    var = jnp.mean(hc * hc, axis=1, keepdims=True)
    return hc * lax.rsqrt(var + LN_EPS) * g + b


def _outproj_kernel(mg_ref, w_ref, x_ref, gate_ref, g_ref, b_ref, sh_ref, sc_ref, rw_ref,
                    xo_ref, uo_ref, lg_ref, acc_ref, *, alpha, n_k):
    k = pl.program_id(1)

    @pl.when(k == 0)
    def _():
        acc_ref[...] = jnp.zeros_like(acc_ref)

    acc_ref[...] += _dot(mg_ref[...], w_ref[...].astype(BF16))

    @pl.when(k == n_k - 1)
    def _():
        xn = _post_norm(x_ref[...], gate_ref[...], acc_ref[...], g_ref[...], b_ref[...], alpha)
        xo_ref[...] = xn
        u = xn * (1.0 + sc_ref[...]) + sh_ref[...]
        uo_ref[...] = u
        u_hi, u_lo = _split_bf16(u)
        w_hi, w_lo = _split_bf16(rw_ref[...])
        lg_ref[...] = _dot(u_hi, w_hi) + _dot(u_lo, w_hi) + _dot(u_hi, w_lo)


def _outproj_norm(merged, w_out, layer, x2, mod_l, ln_g, ln_b, router_w, alpha, seq, tm=512, tk=512):
    n, d = x2.shape
    e = router_w.shape[2]
    tm = min(tm, seq)
    tps = seq // tm
    n_k = d // tk
    kern = functools.partial(_outproj_kernel, alpha=alpha, n_k=n_k)
    vec = lambda arr: pl.BlockSpec((None, 1, d), lambda i, k: (layer, 0, 0))
    return pl.pallas_call(
        kern,
        out_shape=(jax.ShapeDtypeStruct((n, d), F32), jax.ShapeDtypeStruct((n, d), F32),
                   jax.ShapeDtypeStruct((n, e), F32)),
        grid=(n // tm, n_k),
        in_specs=[pl.BlockSpec((tm, tk), lambda i, k: (i, k)),
                  pl.BlockSpec((None, tk, d), lambda i, k: (layer, k, 0)),
                  pl.BlockSpec((tm, d), lambda i, k: (i, 0)),
                  _mod_spec(2, d, tps), vec(ln_g), vec(ln_b), _mod_spec(3, d, tps), _mod_spec(4, d, tps),
                  pl.BlockSpec((None, d, e), lambda i, k: (layer, 0, 0))],
        out_specs=(pl.BlockSpec((tm, d), lambda i, k: (i, 0)), pl.BlockSpec((tm, d), lambda i, k: (i, 0)),
                   pl.BlockSpec((tm, e), lambda i, k: (i, 0))),
        scratch_shapes=[pltpu.VMEM((tm, d), F32)],
        compiler_params=_cparams(2),
        name="out_proj_norm",
    )(merged, w_out, x2, mod_l, ln_g.reshape(-1, 1, d), ln_b.reshape(-1, 1, d), mod_l, mod_l, router_w)


def _route_kernel(lg_ref, bias_ref, idx_ref, w_ref, pos_ref, cnt_ref, carry_ref, *, tm):
    i = pl.program_id(0)

    @pl.when(i == 0)
    def _():
        carry_ref[...] = jnp.zeros_like(carry_ref)

    aff = jax.nn.sigmoid(lg_ref[...])
    e = aff.shape[1]
    lane = lax.broadcasted_iota(I32, (tm, e), 1).astype(F32)
    work = aff + bias_ref[...]
    picks = []
    chosen_f = jnp.zeros((tm, e), F32)
    for _ in range(TOP_K):
        best = jnp.max(work, axis=1, keepdims=True)
        pick = jnp.min(jnp.where(work == best, lane, float(e)), axis=1, keepdims=True)
        onehot = lane == pick
        picks.append((pick, onehot))
        chosen_f = jnp.where(onehot, 1.0, chosen_f)
        work = jnp.where(onehot, NEG_INF, work)

    sel_aff = chosen_f * aff
    dense_w = sel_aff / jnp.sum(sel_aff, axis=1, keepdims=True) * ROUTED_SCALE
    r = lax.broadcasted_iota(I32, (tm, tm), 0)
    c = lax.broadcasted_iota(I32, (tm, tm), 1)
    earlier = (c < r).astype(BF16)
    rank = _dot(earlier, chosen_f.astype(BF16)) + carry_ref[...]
    carry_ref[...] += jnp.sum(chosen_f, axis=0, keepdims=True)
    cnt_ref[...] = carry_ref[...].astype(I32)

    slot = lax.broadcasted_iota(I32, (tm, TOP_K), 1)
    idx8 = jnp.zeros((tm, TOP_K), F32)
    w8 = jnp.zeros((tm, TOP_K), F32)
    pos8 = jnp.zeros((tm, TOP_K), F32)
    for kk, (pick, onehot) in enumerate(picks):
        idx8 = jnp.where(slot == kk, pick, idx8)
        w8 = jnp.where(slot == kk, jnp.sum(jnp.where(onehot, dense_w, 0.0), axis=1, keepdims=True), w8)
        pos8 = jnp.where(slot == kk, jnp.sum(jnp.where(onehot, rank, 0.0), axis=1, keepdims=True), pos8)
    idx_ref[...] = idx8.astype(I32)
    w_ref[...] = w8
    pos_ref[...] = pos8.astype(I32)


def _route(logits, router_bias, layer, tm=512):
    n, e = logits.shape
    kern = functools.partial(_route_kernel, tm=tm)
    return pl.pallas_call(
        kern,
        out_shape=(jax.ShapeDtypeStruct((n, TOP_K), I32), jax.ShapeDtypeStruct((n, TOP_K), F32),
                   jax.ShapeDtypeStruct((n, TOP_K), I32), jax.ShapeDtypeStruct((1, e), I32)),
        grid=(n // tm,),
        in_specs=[pl.BlockSpec((tm, e), lambda i: (i, 0)),
                  pl.BlockSpec((None, 1, e), lambda i: (layer, 0, 0))],
        out_specs=(pl.BlockSpec((tm, TOP_K), lambda i: (i, 0)), pl.BlockSpec((tm, TOP_K), lambda i: (i, 0)),
                   pl.BlockSpec((tm, TOP_K), lambda i: (i, 0)), pl.BlockSpec((1, e), lambda i: (0, 0))),
        scratch_shapes=[pltpu.VMEM((1, e), F32)],
        compiler_params=_cparams(1),
        name="route",
    )(logits, router_bias.reshape(-1, 1, e))


def _experts_kernel(be_ref, nu_ref, tok_ref, u_hbm, wg_ref, wu_ref, wd_ref, y_ref,
                    idx_smem, xbuf, wg_bf, wu_bf, wd_bf, idx_sem, row_sem, *, tr, n_blocks):
    i = pl.program_id(0)
    n_used = nu_ref[0]
    slot = i % 2

    def gather(blk, s):
        cp = pltpu.make_async_copy(tok_ref.at[blk], idx_smem.at[s], idx_sem)
        cp.start()
        cp.wait()

        def issue(rr, carry):
            tok = idx_smem[s, rr]
            pltpu.make_async_copy(u_hbm.at[pl.ds(tok, 1)], xbuf.at[s, pl.ds(rr, 1)], row_sem.at[s]).start()
            return carry

        lax.fori_loop(0, tr, issue, 0)

    @pl.when(i == 0)
    def _():
        gather(0, 0)

    @pl.when(i + 1 < n_used)
    def _():
        gather(i + 1, 1 - slot)

    changed = jnp.logical_or(i == 0, be_ref[i] != be_ref[jnp.maximum(i - 1, 0)])

    @pl.when(changed)
    def _():
        wg_bf[...] = wg_ref[...].astype(BF16)
        wu_bf[...] = wu_ref[...].astype(BF16)
        wd_bf[...] = wd_ref[...].astype(BF16)

    @pl.when(i < n_used)
    def _():
        pltpu.make_async_copy(u_hbm.at[pl.ds(0, tr)], xbuf.at[slot], row_sem.at[slot]).wait()
        xb = xbuf[slot].astype(BF16)
        g = _dot(xb, wg_bf[...])
        h = (g * jax.nn.sigmoid(g)) * _dot(xb, wu_bf[...])
        y_ref[...] = _dot(h.astype(BF16), wd_bf[...])

    @pl.when(i >= n_used)
    def _():
        y_ref[...] = jnp.zeros_like(y_ref)


def _experts(u2, row_tok, block_e, n_used, w_gate, w_up, w_down, layer, tr):
    n, d = u2.shape
    n_blocks = row_tok.shape[0]
    hid = w_gate.shape[3]
    kern = functools.partial(_experts_kernel, tr=tr, n_blocks=n_blocks)
    grid_spec = pltpu.PrefetchScalarGridSpec(
        num_scalar_prefetch=2,
        grid=(n_blocks,),
        in_specs=[pl.BlockSpec((n_blocks, tr), lambda i, be, nu: (0, 0)),
                  pl.BlockSpec(memory_space=pl.ANY),
                  pl.BlockSpec((None, None, d, hid), lambda i, be, nu: (layer, be[i], 0, 0)),
                  pl.BlockSpec((None, None, d, hid), lambda i, be, nu: (layer, be[i], 0, 0)),
                  pl.BlockSpec((None, None, hid, d), lambda i, be, nu: (layer, be[i], 0, 0))],
        out_specs=pl.BlockSpec((tr, d), lambda i, be, nu: (i, 0)),
        scratch_shapes=[pltpu.SMEM((2, tr), I32), pltpu.VMEM((2, tr, d), F32),
                        pltpu.VMEM((d, hid), BF16), pltpu.VMEM((d, hid), BF16), pltpu.VMEM((hid, d), BF16),
                        pltpu.SemaphoreType.DMA(()), pltpu.SemaphoreType.DMA((2,))])
    return pl.pallas_call(
        kern,
        out_shape=jax.ShapeDtypeStruct((n_blocks * tr, d), F32),
        grid_spec=grid_spec,
        compiler_params=_cparams(1),
        name="routed_experts",
    )(block_e, n_used, row_tok, u2, w_gate, w_up, w_down)


def _shared_kernel(u_ref, wg_ref, wu_ref, wd_ref, o_ref, wg_bf, wu_bf, wd_bf):
    @pl.when(pl.program_id(0) == 0)
    def _():
        wg_bf[...] = wg_ref[...].astype(BF16)
        wu_bf[...] = wu_ref[...].astype(BF16)
        wd_bf[...] = wd_ref[...].astype(BF16)

    xb = u_ref[...].astype(BF16)
    g = _dot(xb, wg_bf[...])
    h = (g * jax.nn.sigmoid(g)) * _dot(xb, wu_bf[...])
    o_ref[...] = _dot(h.astype(BF16), wd_bf[...])


def _shared_expert(u2, sh_gate, sh_up, sh_down, layer, tm=512):
    n, d = u2.shape
    hid = sh_gate.shape[2]
    return pl.pallas_call(
        _shared_kernel,
        out_shape=jax.ShapeDtypeStruct((n, d), F32),
        grid=(n // tm,),
        in_specs=[pl.BlockSpec((tm, d), lambda i: (i, 0)),
                  pl.BlockSpec((None, d, hid), lambda i: (layer, 0, 0)),
                  pl.BlockSpec((None, d, hid), lambda i: (layer, 0, 0)),
                  pl.BlockSpec((None, hid, d), lambda i: (layer, 0, 0))],
        out_specs=pl.BlockSpec((tm, d), lambda i: (i, 0)),
        scratch_shapes=[pltpu.VMEM((d, hid), BF16), pltpu.VMEM((d, hid), BF16), pltpu.VMEM((hid, d), BF16)],
        compiler_params=_cparams(1),
        name="shared_expert",
    )(u2, sh_gate, sh_up, sh_down)


def _combine_kernel(dest_ref, y_hbm, w8_ref, shd_ref, x_ref, gate_ref, g_ref, b_ref, sh_ref, sc_ref,
                    xo_ref, uo_ref, idx_smem, ybuf, idx_sem, row_sem, *, tm, n_tiles, alpha):
    i = pl.program_id(0)
    slot = i % 2

    def gather(tile, s):
        cp = pltpu.make_async_copy(dest_ref.at[tile], idx_smem.at[s], idx_sem)
        cp.start()
        cp.wait()

        def issue(rr, carry):
            for kk in range(TOP_K):
                src = idx_smem[s, rr * TOP_K + kk]
                pltpu.make_async_copy(y_hbm.at[pl.ds(src, 1)], ybuf.at[s, kk, pl.ds(rr, 1)],
                                      row_sem.at[s]).start()
            return carry

        lax.fori_loop(0, tm, issue, 0)

    @pl.when(i == 0)
    def _():
        gather(0, 0)

    @pl.when(i + 1 < n_tiles)
    def _():
        gather(i + 1, 1 - slot)

    for kk in range(TOP_K):
        pltpu.make_async_copy(y_hbm.at[pl.ds(0, tm)], ybuf.at[slot, kk], row_sem.at[slot]).wait()
    w8 = w8_ref[...]
    ffn = shd_ref[...]
    for kk in range(TOP_K):
        ffn = ffn + w8[:, kk:kk + 1] * ybuf[slot, kk]
    xn = _post_norm(x_ref[...], gate_ref[...], ffn, g_ref[...], b_ref[...], alpha)
    xo_ref[...] = xn
    uo_ref[...] = (xn * (1.0 + sc_ref[...]) + sh_ref[...]).astype(uo_ref.dtype)


def _combine_norm(dest, y_sorted, w8, shared, x2, mod_l, mod_next, ln_g, ln_b, layer, alpha, seq, tm=128):
    n, d = x2.shape
    n_tiles = n // tm
    tm = min(tm, seq)
    tps = seq // tm
    kern = functools.partial(_combine_kernel, tm=tm, n_tiles=n_tiles, alpha=alpha)
    row = lambda: pl.BlockSpec((tm, d), lambda i: (i, 0))
    vec = lambda: pl.BlockSpec((None, 1, d), lambda i: (layer, 0, 0))
    return pl.pallas_call(
        kern,
        out_shape=(jax.ShapeDtypeStruct((n, d), F32), jax.ShapeDtypeStruct((n, d), BF16)),
        grid=(n_tiles,),
        in_specs=[pl.BlockSpec((n_tiles, tm * TOP_K), lambda i: (0, 0)),
                  pl.BlockSpec(memory_space=pl.ANY),
                  pl.BlockSpec((tm, TOP_K), lambda i: (i, 0)),
                  row(), row(), _mod_spec(5, d, tps), vec(), vec(),
                  _mod_spec(0, d, tps), _mod_spec(1, d, tps)],
        out_specs=(row(), row()),
        scratch_shapes=[pltpu.SMEM((2, tm * TOP_K), I32), pltpu.VMEM((2, TOP_K, tm, d), F32),
                        pltpu.SemaphoreType.DMA(()), pltpu.SemaphoreType.DMA((2,))],
        compiler_params=_cparams(1),
        name="combine_norm",
    )(dest.reshape(n_tiles, tm * TOP_K), y_sorted, w8, shared, x2, mod_l,
      ln_g.reshape(-1, 1, d), ln_b.reshape(-1, 1, d), mod_next, mod_next)


def _dispatch_plan(idx8, pos8, counts, tr):
    n = idx8.shape[0]
    a = n * TOP_K
    n_blocks = -(-(a + N_EXPERTS * (tr - 1)) // tr)
    padded = (counts + tr - 1) // tr * tr
    end_padded = jnp.cumsum(padded)
    start_padded = end_padded - padded
    dest = start_padded[idx8] + pos8
    tok = jnp.broadcast_to(jnp.arange(n, dtype=I32)[:, None], (n, TOP_K))
    row_tok = jnp.zeros((n_blocks * tr,), I32).at[dest.reshape(-1)].set(tok.reshape(-1))
    block_start = jnp.arange(n_blocks, dtype=I32) * tr
    block_e = jnp.minimum(jnp.searchsorted(end_padded, block_start, side="right"), N_EXPERTS - 1).astype(I32)
    n_used = (end_padded[-1] // tr).astype(I32).reshape(1)
    return dest.astype(I32), row_tok.reshape(n_blocks, tr), block_e, n_used


def kernel(x, c, ada_w, ada_b, w_in, gate_b, pool_w, pool_scale, conv_w, w_branch, w_out, ln1_g, ln1_b,
           router_w, router_bias, exp_w_gate, exp_w_up, exp_w_down, sh_w_gate, sh_w_up, sh_w_down, ln2_g, ln2_b):
    b, seq, d = x.shape
    depth = ada_w.shape[0]
    n = b * seq
    alpha = (2 * depth) ** 0.25
    expert_rows = 128

    w = BRANCH_WIDTH
    cb = w // LANES
    sq, sk, sv = 0, cb, 2 * cb
    pool_col = 3 * w
    mq_col = 4 * w
    mq, mk, mv = mq_col // LANES, mq_col // LANES + cb, mq_col // LANES + 2 * cb
    conv_col = 7 * w
    gate_col = 10 * w

    mod = _ada(c, ada_w, ada_b)
    cos_t, sin_t = _rope_tables(seq)
    x2 = x.reshape(n, d)
    u = _modulate(x2, mod[0], seq)
    for l in range(depth):
        proj = _inproj(u, w_in, l, cos_t, sin_t, seq, rot_col=mq_col)
        proj3 = proj.reshape(b, seq, -1)
        y_sb = _sb_attention(proj3, sq, sk, sv).reshape(n, w)
        y_mb = _moba_attention(proj3, mq, mk, mv).reshape(n, w)
        y_pool, y_cv = _local_mixers(proj, pool_w[l], pool_scale[l], conv_w[l], seq,
                                     pool_col // w, conv_col // w, conv_col // w + 1, conv_col // w + 2)
        merged = _merge((y_sb, y_pool, y_mb, y_cv), proj, w_branch, gate_b[l], l, gate_col)
        x2, u2, logits = _outproj_norm(merged, w_out, l, x2, mod[l], ln1_g, ln1_b, router_w, alpha, seq)
        idx8, w8, pos8, counts = _route(logits, router_bias, l)
        dest, row_tok, block_e, n_used = _dispatch_plan(idx8, pos8, counts[0], expert_rows)
        y_sorted = _experts(u2, row_tok, block_e, n_used, exp_w_gate, exp_w_up, exp_w_down, l, expert_rows)
        shared = _shared_expert(u2, sh_w_gate, sh_w_up, sh_w_down, l)
        x2, u = _combine_norm(dest, y_sorted, w8, shared, x2, mod[l], mod[(l + 1) % depth],
                              ln2_g, ln2_b, l, alpha, seq)
    return x2.reshape(b, seq, d)
```

```python
import functools

import numpy as np
import jax
import jax.numpy as jnp
from jax import lax
from jax.experimental import pallas as pl
from jax.experimental.pallas import tpu as pltpu

F32 = jnp.float32
BF16 = jnp.bfloat16
I32 = jnp.int32
U32 = jnp.uint32

HEAD_DIM = 64
N_BRANCH = 4
BRANCH_WIDTH = 512
POOL_WINDOWS = (2, 4, 8, 16)
POOL_GROUP = 128
POOL_HALO = 16
CONV_K = 3
MOBA_BLOCK = 256
MOBA_TOPK = 3
ROPE_THETA = 10000.0
N_EXPERTS = 64
TOP_K = 8
EXPERT_HIDDEN = 384
ROUTED_SCALE = 2.5
LN_EPS = 1e-5
NEG_INF = float("-inf")
SB_DEAD_LOG = -100.0
MASKED = -1e30

LANES = 128
VMEM_LIMIT = 56 * 1024 * 1024


def _cparams(n_axes, vmem=VMEM_LIMIT):
    return pltpu.CompilerParams(dimension_semantics=("arbitrary",) * n_axes, vmem_limit_bytes=vmem)


def _dot(a, b):
    return jnp.dot(a, b, preferred_element_type=F32)


def _dot_nt(a, b):
    return lax.dot_general(a, b, (((1,), (1,)), ((), ())), preferred_element_type=F32)


def _split_bf16(x):
    hi = x.astype(BF16)
    lo = (x - hi.astype(F32)).astype(BF16)
    return hi, lo


def _ada_kernel(c_ref, w_ref, b_ref, o_ref):
    c = c_ref[...]
    cond = c * jax.nn.sigmoid(c)
    o_ref[...] = jnp.dot(cond, w_ref[...], preferred_element_type=F32,
                         precision=lax.Precision.HIGHEST) + b_ref[...]


def _ada(c, ada_w, ada_b):
    depth, d, d6 = ada_w.shape
    b = c.shape[0]
    rows = 8
    tn = 1024
    c8 = jnp.zeros((rows, d), F32).at[:b].set(c)
    out = pl.pallas_call(
        _ada_kernel,
        out_shape=jax.ShapeDtypeStruct((depth, rows, d6), F32),
        grid=(depth, d6 // tn),
        in_specs=[pl.BlockSpec((rows, d), lambda l, n: (0, 0)),
                  pl.BlockSpec((None, d, tn), lambda l, n: (l, 0, n)),
                  pl.BlockSpec((None, 1, tn), lambda l, n: (l, 0, n))],
        out_specs=pl.BlockSpec((None, rows, tn), lambda l, n: (l, 0, n)),
        compiler_params=_cparams(2),
        name="ada_mod",
    )(c8, ada_w, ada_b.reshape(depth, 1, d6))
    return out[:, :b].reshape(depth, b, 6, 1, d)


def _mod_spec(which, d, tiles_per_seq):
    return pl.BlockSpec((None, None, 1, d), lambda i, *_: (i // tiles_per_seq, which, 0, 0))


def _modulate_kernel(x_ref, sh_ref, sc_ref, u_ref):
    u_ref[...] = (x_ref[...] * (1.0 + sc_ref[...]) + sh_ref[...]).astype(u_ref.dtype)


def _modulate(x2, mod_l, seq, tm=512):
    n, d = x2.shape
    tm = min(tm, seq)
    tps = seq // tm
    return pl.pallas_call(
        _modulate_kernel,
        out_shape=jax.ShapeDtypeStruct((n, d), BF16),
        grid=(n // tm,),
        in_specs=[pl.BlockSpec((tm, d), lambda i: (i, 0)), _mod_spec(0, d, tps), _mod_spec(1, d, tps)],
        out_specs=pl.BlockSpec((tm, d), lambda i: (i, 0)),
        compiler_params=_cparams(1),
        name="modulate",
    )(x2, mod_l, mod_l)


def _inproj_kernel(u_ref, w_ref, cos_ref, sin_ref, o_ref, wbf_ref, *, rot_tile, tn):
    n = pl.program_id(0)
    m = pl.program_id(1)

    @pl.when(m == 0)
    def _():
        wbf_ref[...] = w_ref[...].astype(BF16)

    acc = _dot(u_ref[...], wbf_ref[...])

    @pl.when(n == rot_tile)
    def _():
        reps = tn // LANES
        cos = jnp.concatenate([cos_ref[...]] * reps, axis=1)
        sin = jnp.concatenate([sin_ref[...]] * reps, axis=1)
        lane = lax.broadcasted_iota(I32, acc.shape, 1)
        first_half = (lane % HEAD_DIM) < (HEAD_DIM // 2)
        half = HEAD_DIM // 2
        partner = jnp.where(first_half, pltpu.roll(acc, tn - half, 1), pltpu.roll(acc, half, 1))
        o_ref[...] = (acc * cos + partner * sin).astype(o_ref.dtype)

    @pl.when(n != rot_tile)
    def _():
        o_ref[...] = acc.astype(o_ref.dtype)


def _inproj(u, w_in, layer, cos_t, sin_t, seq, rot_col, tm=1024, tn=1024):
    n, d = u.shape
    cols = w_in.shape[2]
    tm = min(tm, seq)
    tps = seq // tm
    assert rot_col % tn == 0
    kern = functools.partial(_inproj_kernel, rot_tile=rot_col // tn, tn=tn)
    return pl.pallas_call(
        kern,
        out_shape=jax.ShapeDtypeStruct((n, cols), BF16),
        grid=(cols // tn, n // tm),
        in_specs=[pl.BlockSpec((tm, d), lambda j, i: (i, 0)),
                  pl.BlockSpec((None, d, tn), lambda j, i: (layer, 0, j)),
                  pl.BlockSpec((tm, LANES), lambda j, i: (i % tps, 0)),
                  pl.BlockSpec((tm, LANES), lambda j, i: (i % tps, 0))],
        out_specs=pl.BlockSpec((tm, tn), lambda j, i: (i, j)),
        scratch_shapes=[pltpu.VMEM((d, tn), BF16)],
        compiler_params=_cparams(2),
        name="in_proj",
    )(u, w_in, cos_t, sin_t)


def _rope_tables(seq):
    half = HEAD_DIM // 2
    inv_freq = ROPE_THETA ** (-jnp.arange(half, dtype=F32) / half)
    ang = jnp.arange(seq, dtype=F32)[:, None] * inv_freq[None, :]
    cos, sin = jnp.cos(ang), jnp.sin(ang)
    cos_h = jnp.concatenate([cos, cos], axis=1)
    sin_h = jnp.concatenate([-sin, sin], axis=1)
    reps = LANES // HEAD_DIM
    return jnp.tile(cos_h, (1, reps)), jnp.tile(sin_h, (1, reps))


def _sb_kernel(q_ref, k_ref, v_ref, o_ref, *, tq, tk):
    qi = pl.program_id(2)
    q0 = qi * tq
    r = lax.broadcasted_iota(I32, (tk, tk), 0)
    c = lax.broadcasted_iota(I32, (tk, tk), 1)
    later = (r > c).astype(BF16)
    heads = LANES // HEAD_DIM
    qs = [q_ref[:, h * HEAD_DIM:(h + 1) * HEAD_DIM] * (HEAD_DIM ** -0.5) for h in range(heads)]

    def block(j0, h, carry, acc, masked):
        k = k_ref[pl.ds(j0, tk), h * HEAD_DIM:(h + 1) * HEAD_DIM]
        v = v_ref[pl.ds(j0, tk), h * HEAD_DIM:(h + 1) * HEAD_DIM]
        z = _dot_nt(qs[h], k)
        softplus = jnp.maximum(z, 0.0) + jnp.log(1.0 + jnp.exp(-jnp.abs(z)))
        log_keep = -softplus
        if masked:
            qpos = q0 + lax.broadcasted_iota(I32, (tq, tk), 0)
            kpos = j0 + lax.broadcasted_iota(I32, (tq, tk), 1)
            past = kpos < qpos
            log_keep = jnp.where(past, log_keep, 0.0)
        hi, lo = _split_bf16(log_keep)
        between = _dot(hi, later) + _dot(lo, later) + carry
        w = jnp.exp((z - softplus) + between)
        if masked:
            w = jnp.where(past, w, 0.0)
        acc = acc + _dot(w.astype(BF16), v)
        carry = carry + jnp.sum(log_keep, axis=1, keepdims=True)
        return carry, acc

    state = [(jnp.zeros((tq, 1), F32), jnp.zeros((tq, HEAD_DIM), F32)) for _ in range(heads)]
    n_diag = tq // tk
    for d in range(n_diag - 1, -1, -1):
        j0 = pl.multiple_of(q0 + d * tk, tk)
        state = [block(j0, h, *state[h], True) for h in range(heads)]

    def live(pairs):
        return functools.reduce(jnp.maximum, [jnp.max(c) for c, _ in pairs])

    def cond(st):
        return jnp.logical_and(st[0] < qi * n_diag, st[1] > SB_DEAD_LOG)

    def body(st):
        jj = st[0]
        j0 = pl.multiple_of(q0 - (jj + 1) * tk, tk)
        new = [block(j0, h, st[2 + 2 * h], st[3 + 2 * h], False) for h in range(heads)]
        return (jj + 1, live(new)) + tuple(x for pair in new for x in pair)

    final = lax.while_loop(cond, body, (jnp.int32(0), live(state)) + tuple(x for pair in state for x in pair))
    o_ref[...] = jnp.concatenate([final[3 + 2 * h] for h in range(heads)], axis=1).astype(o_ref.dtype)


def _sb_attention(proj3, q_blk, k_blk, v_blk, tq=256, tk=128):
    b, seq, _ = proj3.shape
    pairs = BRANCH_WIDTH // LANES
    kern = functools.partial(_sb_kernel, tq=tq, tk=tk)
    return pl.pallas_call(
        kern,
        out_shape=jax.ShapeDtypeStruct((b, seq, BRANCH_WIDTH), BF16),
        grid=(b, pairs, seq // tq),
        in_specs=[pl.BlockSpec((None, tq, LANES), lambda bi, p, qi: (bi, qi, q_blk + p)),
                  pl.BlockSpec((None, seq, LANES), lambda bi, p, qi: (bi, 0, k_blk + p)),
                  pl.BlockSpec((None, seq, LANES), lambda bi, p, qi: (bi, 0, v_blk + p))],
        out_specs=pl.BlockSpec((None, tq, LANES), lambda bi, p, qi: (bi, qi, p)),
        compiler_params=_cparams(3),
        name="sb_attention",
    )(proj3, proj3, proj3)


def _moba_kernel(q_ref, k_ref, v_ref, o_ref, kmean_ref, kt_ref, vaug_ref, s_ref, *, n_kb):
    own = pl.program_id(2)
    tq = MOBA_BLOCK
    heads = LANES // HEAD_DIM

    @pl.when(own == 0)
    def _():
        ones = jnp.ones((MOBA_BLOCK, HEAD_DIM), BF16)
        for jb in range(n_kb):
            rows = slice(jb * MOBA_BLOCK, (jb + 1) * MOBA_BLOCK)
            kb = k_ref[rows, :]
            kmean_ref[jb:jb + 1, :] = jnp.mean(kb.astype(F32), axis=0, keepdims=True)
            kt_ref[jb] = kb.T
            vb = v_ref[rows, :]
            for h in range(heads):
                vaug_ref[h, rows, :] = jnp.concatenate([vb[:, h * HEAD_DIM:(h + 1) * HEAD_DIM], ones], axis=1)

    blk = lax.broadcasted_iota(I32, (n_kb, tq), 0)
    row = lax.broadcasted_iota(I32, (tq, MOBA_BLOCK), 0)
    col = lax.broadcasted_iota(I32, (tq, MOBA_BLOCK), 1)
    causal = col <= row
    block_id = lax.broadcasted_iota(I32, (n_kb, MOBA_BLOCK), 0)

    qs, sel_bias, run_max = [], [], []
    for h in range(heads):
        hs = slice(h * HEAD_DIM, (h + 1) * HEAD_DIM)
        q = q_ref[:, hs] * (HEAD_DIM ** -0.5)
        qs.append(q)
        km_hi, km_lo = _split_bf16(kmean_ref[:, hs])
        gate = _dot_nt(km_hi, q) + _dot_nt(km_lo, q)
        beaten = jnp.zeros((n_kb, tq), I32)
        for jp in range(n_kb):
            g = gate[jp:jp + 1, :]
            beats = (g > gate) | ((g == gate) & (jp < blk))
            beaten = beaten + jnp.where(beats, 1, 0) * (jp < own).astype(I32)
        chosen = (beaten < MOBA_TOPK) & (blk < own)
        sel_bias.append(jnp.where(chosen, 0.0, MASKED).T.astype(BF16))
        s = jnp.where(causal, _dot(q, kt_ref[own, hs, :]), NEG_INF)
        s_ref[h, own] = s
        run_max.append(s)

    def score_pass(j, running):
        pick = (block_id == j).astype(BF16)
        out = []
        for h in range(heads):
            hs = slice(h * HEAD_DIM, (h + 1) * HEAD_DIM)
            s = _dot(qs[h], kt_ref[j, hs, :]) + _dot(sel_bias[h], pick)
            s_ref[h, j] = s
            out.append(jnp.maximum(running[h], s))
        return tuple(out)

    run_max = lax.fori_loop(0, own, score_pass, tuple(run_max))
    row_max = [jnp.max(r, axis=1, keepdims=True) for r in run_max]

    def softmax_pass(j, accs):
        j0 = pl.multiple_of(j * MOBA_BLOCK, MOBA_BLOCK)
        out = []
        for h in range(heads):
            p = jnp.exp(s_ref[h, j] - row_max[h])
            out.append(accs[h] + _dot(p.astype(BF16), vaug_ref[h, pl.ds(j0, MOBA_BLOCK), :]))
        return tuple(out)

    accs = lax.fori_loop(0, own + 1, softmax_pass, tuple(jnp.zeros((tq, LANES), F32) for _ in range(heads)))
    o_ref[...] = jnp.concatenate([a[:, :HEAD_DIM] / a[:, HEAD_DIM:] for a in accs], axis=1).astype(o_ref.dtype)


def _moba_attention(proj3, q_blk, k_blk, v_blk):
    b, seq, _ = proj3.shape
    assert seq % MOBA_BLOCK == 0
    n_kb = seq // MOBA_BLOCK
    pairs = BRANCH_WIDTH // LANES
    kern = functools.partial(_moba_kernel, n_kb=n_kb)
    return pl.pallas_call(
        kern,
        out_shape=jax.ShapeDtypeStruct((b, seq, BRANCH_WIDTH), BF16),
        grid=(b, pairs, n_kb),
        in_specs=[pl.BlockSpec((None, MOBA_BLOCK, LANES), lambda bi, p, qi: (bi, qi, q_blk + p)),
                  pl.BlockSpec((None, seq, LANES), lambda bi, p, qi: (bi, 0, k_blk + p)),
                  pl.BlockSpec((None, seq, LANES), lambda bi, p, qi: (bi, 0, v_blk + p))],
        out_specs=pl.BlockSpec((None, MOBA_BLOCK, LANES), lambda bi, p, qi: (bi, qi, p)),
        scratch_shapes=[pltpu.VMEM((n_kb, LANES), F32),
                        pltpu.VMEM((n_kb, LANES, MOBA_BLOCK), BF16),
                        pltpu.VMEM((LANES // HEAD_DIM, seq, LANES), BF16),
                        pltpu.VMEM((LANES // HEAD_DIM, n_kb, MOBA_BLOCK, MOBA_BLOCK), F32)],
        compiler_params=_cparams(3),
        name="moba_attention",
    )(proj3, proj3, proj3)


def _local_kernel(p_ref, ph_ref, ch_ref, chh_ref, cb_ref, cc_ref, cch_ref, pw_ref, ps_ref, cw_ref,
                  yp_ref, yc_ref, *, tm, tiles_per_seq):
    i = pl.program_id(0)
    seq_tile = i % tiles_per_seq
    has_left = seq_tile > 0

    def with_halo(halo_ref, body_ref):
        halo = jnp.where(has_left, halo_ref[...].astype(F32), 0.0)
        return jnp.concatenate([halo, body_ref[...].astype(F32)], axis=0)

    x = with_halo(ph_ref, p_ref)
    pos = seq_tile * tm + lax.broadcasted_iota(I32, (tm, 1), 0)
    for g, win in enumerate(POOL_WINDOWS):
        gs = slice(g * POOL_GROUP, (g + 1) * POOL_GROUP)
        xg = x[:, gs]
        s, k = xg, 1
        while k < win:
            s = s[k:] + s[:-k]
            k *= 2
        start = POOL_HALO + 1 - win
        window_sum = s[start:start + tm]
        count = jnp.minimum(pos + 1, win).astype(F32)
        mixed = window_sum / count - xg[POOL_HALO:]
        y = _dot(mixed.astype(BF16), pw_ref[g].astype(BF16)) * ps_ref[:, gs]
        yp_ref[:, gs] = y.astype(yp_ref.dtype)

    z = with_halo(chh_ref, ch_ref) * with_halo(cch_ref, cc_ref)
    y = jnp.zeros((tm, z.shape[1]), F32)
    for k in range(CONV_K):
        off = POOL_HALO - (CONV_K - 1) + k
        y = y + cw_ref[k:k + 1, :] * z[off:off + tm]
    yc_ref[...] = (cb_ref[...].astype(F32) * y).astype(yc_ref.dtype)


def _local_mixers(proj, pool_w, pool_scale, conv_w, seq, pool_blk, ch_blk, cb_blk, cc_blk, tm=512):
    n = proj.shape[0]
    w = BRANCH_WIDTH
    tm = min(tm, seq)
    tps = seq // tm
    hpt = tm // POOL_HALO

    def body(blk):
        return pl.BlockSpec((tm, w), lambda i: (i, blk))

    def halo(blk):
        return pl.BlockSpec((POOL_HALO, w), lambda i: (jnp.maximum(i * hpt - 1, 0), blk))

    kern = functools.partial(_local_kernel, tm=tm, tiles_per_seq=tps)
    return pl.pallas_call(
        kern,
        out_shape=(jax.ShapeDtypeStruct((n, w), BF16), jax.ShapeDtypeStruct((n, w), BF16)),
        grid=(n // tm,),
        in_specs=[body(pool_blk), halo(pool_blk), body(ch_blk), halo(ch_blk), body(cb_blk),
                  body(cc_blk), halo(cc_blk),
                  pl.BlockSpec(pool_w.shape, lambda i: (0, 0, 0)),
                  pl.BlockSpec((1, w), lambda i: (0, 0)),
                  pl.BlockSpec((CONV_K, w), lambda i: (0, 0))],
        out_specs=(pl.BlockSpec((tm, w), lambda i: (i, 0)), pl.BlockSpec((tm, w), lambda i: (i, 0))),
        compiler_params=_cparams(1),
        name="pool_conv",
    )(proj, proj, proj, proj, proj, proj, proj, pool_w, pool_scale.reshape(1, w), conv_w)


def _merge_kernel(b0, b1, b2, b3, g0, g1, g2, g3, wb_ref, gb_ref, o_ref, wbf_ref):
    @pl.when(pl.program_id(1) == 0)
    def _():
        wbf_ref[...] = wb_ref[...].astype(BF16)

    merged = None
    for nb, (br, gr) in enumerate(((b0, g0), (b1, g1), (b2, g2), (b3, g3))):
        lifted = _dot(br[...], wbf_ref[nb])
        gate = jax.nn.sigmoid(gr[...].astype(F32) + gb_ref[nb])
        term = gate * lifted
        merged = term if merged is None else merged + term
    o_ref[...] = merged.astype(o_ref.dtype)


def _merge(branches, proj, w_branch, gate_b, layer, gate_col, tm=1024, tn=512):
    n = proj.shape[0]
    d = w_branch.shape[3]
    assert gate_col % tn == 0 and d % tn == 0
    g_specs = [pl.BlockSpec((tm, tn), functools.partial(lambda j, i, nb: (i, (gate_col + nb * d) // tn + j), nb=nb))
               for nb in range(N_BRANCH)]
    b_specs = [pl.BlockSpec((tm, BRANCH_WIDTH), lambda j, i: (i, 0))] * N_BRANCH
    return pl.pallas_call(
        _merge_kernel,
        out_shape=jax.ShapeDtypeStruct((n, d), BF16),
        grid=(d // tn, n // tm),
        in_specs=b_specs + g_specs + [
            pl.BlockSpec((None, N_BRANCH, BRANCH_WIDTH, tn), lambda j, i: (layer, 0, 0, j)),
            pl.BlockSpec((N_BRANCH, 1, tn), lambda j, i: (0, 0, j))],
        out_specs=pl.BlockSpec((tm, tn), lambda j, i: (i, j)),
        scratch_shapes=[pltpu.VMEM((N_BRANCH, BRANCH_WIDTH, tn), BF16)],
        compiler_params=_cparams(2),
        name="gated_merge",
    )(*branches, proj, proj, proj, proj, w_branch, gate_b.reshape(N_BRANCH, 1, d))


def _post_norm(x, gate, y, g, b, alpha):
    h = alpha * x + (1.0 + gate) * y
    mu = jnp.mean(h, axis=1, keepdims=True)
    hc = h - mu
    var = jnp.mean(hc * hc, axis=1, keepdims=True)
    return hc * lax.rsqrt(var + LN_EPS) * g + b


def _outproj_kernel(mg_ref, w_ref, x_ref, gate_ref, g_ref, b_ref, sh_ref, sc_ref, rw_ref,
                    xo_ref, uo_ref, lg_ref, acc_ref, *, alpha, n_k):
    k = pl.program_id(1)

    @pl.when(k == 0)
    def _():
        acc_ref[...] = jnp.zeros_like(acc_ref)

    acc_ref[...] += _dot(mg_ref[...], w_ref[...].astype(BF16))

    @pl.when(k == n_k - 1)
    def _():
        xn = _post_norm(x_ref[...], gate_ref[...], acc_ref[...], g_ref[...], b_ref[...], alpha)
        xo_ref[...] = xn
        u = xn * (1.0 + sc_ref[...]) + sh_ref[...]
        uo_ref[...] = u
        u_hi, u_lo = _split_bf16(u)
        w_hi, w_lo = _split_bf16(rw_ref[...])
        lg_ref[...] = _dot(u_hi, w_hi) + _dot(u_lo, w_hi) + _dot(u_hi, w_lo)


def _outproj_norm(merged, w_out, layer, x2, mod_l, ln_g, ln_b, router_w, alpha, seq, tm=512, tk=512):
    n, d = x2.shape
    e = router_w.shape[2]
    tm = min(tm, seq)
    tps = seq // tm
    n_k = d // tk
    kern = functools.partial(_outproj_kernel, alpha=alpha, n_k=n_k)
    vec = lambda arr: pl.BlockSpec((None, 1, d), lambda i, k: (layer, 0, 0))
    return pl.pallas_call(
        kern,
        out_shape=(jax.ShapeDtypeStruct((n, d), F32), jax.ShapeDtypeStruct((n, d), F32),
                   jax.ShapeDtypeStruct((n, e), F32)),
        grid=(n // tm, n_k),
        in_specs=[pl.BlockSpec((tm, tk), lambda i, k: (i, k)),
                  pl.BlockSpec((None, tk, d), lambda i, k: (layer, k, 0)),
                  pl.BlockSpec((tm, d), lambda i, k: (i, 0)),
                  _mod_spec(2, d, tps), vec(ln_g), vec(ln_b), _mod_spec(3, d, tps), _mod_spec(4, d, tps),
                  pl.BlockSpec((None, d, e), lambda i, k: (layer, 0, 0))],
        out_specs=(pl.BlockSpec((tm, d), lambda i, k: (i, 0)), pl.BlockSpec((tm, d), lambda i, k: (i, 0)),
                   pl.BlockSpec((tm, e), lambda i, k: (i, 0))),
        scratch_shapes=[pltpu.VMEM((tm, d), F32)],
        compiler_params=_cparams(2),
        name="out_proj_norm",
    )(merged, w_out, x2, mod_l, ln_g.reshape(-1, 1, d), ln_b.reshape(-1, 1, d), mod_l, mod_l, router_w)


def _route_kernel(lg_ref, bias_ref, idx_ref, w_ref, pos_ref, cnt_ref, carry_ref, *, tm):
    i = pl.program_id(0)

    @pl.when(i == 0)
    def _():
        carry_ref[...] = jnp.zeros_like(carry_ref)

    aff = jax.nn.sigmoid(lg_ref[...].T)
    e = aff.shape[0]
    sub = lax.broadcasted_iota(I32, (e, tm), 0).astype(F32)
    work = aff + bias_ref[...]
    picks = []
    chosen_f = jnp.zeros((e, tm), F32)
    for _ in range(TOP_K):
        best = jnp.max(work, axis=0, keepdims=True)
        pick = jnp.min(jnp.where(work == best, sub, float(e)), axis=0, keepdims=True)
        onehot = sub == pick
        picks.append((pick, onehot))
        chosen_f = jnp.where(onehot, 1.0, chosen_f)
        work = jnp.where(onehot, NEG_INF, work)

    sel_aff = chosen_f * aff
    dense_w = sel_aff / jnp.sum(sel_aff, axis=0, keepdims=True) * ROUTED_SCALE
    r = lax.broadcasted_iota(I32, (tm, tm), 0)
    c = lax.broadcasted_iota(I32, (tm, tm), 1)
    earlier = (r < c).astype(BF16)
    rank = _dot(chosen_f.astype(BF16), earlier) + carry_ref[...]
    carry_ref[...] += jnp.sum(chosen_f, axis=1, keepdims=True)
    cnt_ref[...] = carry_ref[...].astype(I32)

    slot = lax.broadcasted_iota(I32, (TOP_K, tm), 0)
    idx8 = jnp.zeros((TOP_K, tm), F32)
    w8 = jnp.zeros((TOP_K, tm), F32)
    pos8 = jnp.zeros((TOP_K, tm), F32)
    for kk, (pick, onehot) in enumerate(picks):
        idx8 = jnp.where(slot == kk, pick, idx8)
        w8 = jnp.where(slot == kk, jnp.sum(jnp.where(onehot, dense_w, 0.0), axis=0, keepdims=True), w8)
        pos8 = jnp.where(slot == kk, jnp.sum(jnp.where(onehot, rank, 0.0), axis=0, keepdims=True), pos8)
    idx_ref[...] = idx8.astype(I32)
    w_ref[...] = w8
    pos_ref[...] = pos8.astype(I32)


def _route(logits, router_bias, layer, tm=512):
    n, e = logits.shape
    kern = functools.partial(_route_kernel, tm=tm)
    slots = lambda: pl.BlockSpec((TOP_K, tm), lambda i: (0, i))
    return pl.pallas_call(
        kern,
        out_shape=(jax.ShapeDtypeStruct((TOP_K, n), I32), jax.ShapeDtypeStruct((TOP_K, n), F32),
                   jax.ShapeDtypeStruct((TOP_K, n), I32), jax.ShapeDtypeStruct((e, 1), I32)),
        grid=(n // tm,),
        in_specs=[pl.BlockSpec((tm, e), lambda i: (i, 0)),
                  pl.BlockSpec((None, e, 1), lambda i: (layer, 0, 0))],
        out_specs=(slots(), slots(), slots(), pl.BlockSpec((e, 1), lambda i: (0, 0))),
        scratch_shapes=[pltpu.VMEM((e, 1), F32)],
        compiler_params=_cparams(1),
        name="route",
    )(logits, router_bias.reshape(-1, e, 1))


def _dispatch_kernel(cnt_ref, start_ref, nu_ref, u_ref, idx_ref, pos_ref, dest_ref, xs_hbm,
                     dest_smem, zrow_ref, zblk_ref, idx_sem, row_sem, zero_sem,
                     *, tm, tr, n_tiles, n_blocks, n_experts):
    i = pl.program_id(0)
    idx = idx_ref[...]
    dest = pos_ref[...]
    for e in range(n_experts):
        dest = dest + jnp.where(idx == e, start_ref[e], 0)
    dest_ref[...] = dest
    idx_cp = pltpu.make_async_copy(dest_ref, dest_smem, idx_sem)
    idx_cp.start()
    idx_cp.wait()

    def issue(g, carry):
        base = pl.multiple_of(g * 8, 8)
        for u in range(8):
            for kk in range(TOP_K):
                dst = dest_smem[kk, base + u]
                pltpu.make_async_copy(u_ref.at[pl.ds(base + u, 1)], xs_hbm.at[pl.ds(dst, 1)], row_sem).start()
        return carry

    lax.fori_loop(0, tm // 8, issue, 0)
    for kk in range(TOP_K):
        pltpu.make_async_copy(u_ref, xs_hbm.at[pl.ds(0, tm)], row_sem).wait()

    @pl.when(i == n_tiles - 1)
    def _():
        zrow_ref[...] = jnp.zeros_like(zrow_ref)
        zblk_ref[...] = jnp.zeros_like(zblk_ref)

        def pad_expert(e, carry):
            cnt = cnt_ref[e]
            first = start_ref[e] + cnt
            n_pad = (tr - cnt % tr) % tr

            def start(r, c):
                pltpu.make_async_copy(zrow_ref.at[pl.ds(0, 1)], xs_hbm.at[pl.ds(first + r, 1)], zero_sem).start()
                return c

            def wait(r, c):
                pltpu.make_async_copy(zrow_ref.at[pl.ds(0, 1)], xs_hbm.at[pl.ds(0, 1)], zero_sem).wait()
                return c

            lax.fori_loop(0, n_pad, start, 0)
            lax.fori_loop(0, n_pad, wait, 0)
            return carry

        lax.fori_loop(0, n_experts, pad_expert, 0)

        def tail_start(blk, c):
            row0 = pl.multiple_of(blk * tr, tr)
            pltpu.make_async_copy(zblk_ref, xs_hbm.at[pl.ds(row0, tr)], zero_sem).start()
            return c

        def tail_wait(blk, c):
            pltpu.make_async_copy(zblk_ref, xs_hbm.at[pl.ds(0, tr)], zero_sem).wait()
            return c

        lax.fori_loop(nu_ref[0], n_blocks, tail_start, 0)
        lax.fori_loop(nu_ref[0], n_blocks, tail_wait, 0)


def _dispatch(u2, idx_t, pos_t, counts, start_padded, n_used, tr, n_blocks, tm=512):
    n, d = u2.shape
    n_tiles = n // tm
    kern = functools.partial(_dispatch_kernel, tm=tm, tr=tr, n_tiles=n_tiles, n_blocks=n_blocks,
                             n_experts=counts.shape[0])
    slots = lambda: pl.BlockSpec((TOP_K, tm), lambda i, *_: (0, i))
    grid_spec = pltpu.PrefetchScalarGridSpec(
        num_scalar_prefetch=3,
        grid=(n_tiles,),
        in_specs=[pl.BlockSpec((tm, d), lambda i, *_: (i, 0)), slots(), slots()],
        out_specs=(slots(), pl.BlockSpec(memory_space=pl.ANY)),
        scratch_shapes=[pltpu.SMEM((TOP_K, tm), I32), pltpu.VMEM((8, d), F32), pltpu.VMEM((tr, d), F32),
                        pltpu.SemaphoreType.DMA(()), pltpu.SemaphoreType.DMA(()), pltpu.SemaphoreType.DMA(())])
    dest_t, xs = pl.pallas_call(
        kern,
        out_shape=(jax.ShapeDtypeStruct((TOP_K, n), I32), jax.ShapeDtypeStruct((n_blocks * tr, d), F32)),
        grid_spec=grid_spec,
        compiler_params=_cparams(1),
        name="dispatch",
    )(counts, start_padded, n_used, u2, idx_t, pos_t)
    return xs, dest_t


def _experts_kernel(be_ref, bs_ref, nu_ref, x_ref, wg_ref, wu_ref, wd_ref, y_ref, wgu_bf, wd_bf, *, hid):
    i = pl.program_id(0)

    @pl.when(i < nu_ref[0])
    def _():
        changed = jnp.logical_or(i == 0, be_ref[i] != be_ref[jnp.maximum(i - 1, 0)])

        @pl.when(changed)
        def _():
            wgu_bf[:, :hid] = wg_ref[...].astype(BF16)
            wgu_bf[:, hid:] = wu_ref[...].astype(BF16)
            wd_bf[...] = wd_ref[...].astype(BF16)

        gu = _dot(x_ref[...].astype(BF16), wgu_bf[...])
        g = gu[:, :hid]
        h = (g * jax.nn.sigmoid(g)) * gu[:, hid:]
        y_ref[...] = _dot(h.astype(BF16), wd_bf[...])

    @pl.when(i >= nu_ref[0])
    def _():
        y_ref[...] = jnp.zeros_like(y_ref)


def _experts(xs, block_e, block_src, n_used, w_gate, w_up, w_down, layer, tr):
    rows, d = xs.shape
    n_blocks = rows // tr
    hid = w_gate.shape[3]
    kern = functools.partial(_experts_kernel, hid=hid)
    grid_spec = pltpu.PrefetchScalarGridSpec(
        num_scalar_prefetch=3,
        grid=(n_blocks,),
        in_specs=[pl.BlockSpec((tr, d), lambda i, be, bs, nu: (bs[i], 0)),
                  pl.BlockSpec((None, None, d, hid), lambda i, be, bs, nu: (layer, be[i], 0, 0)),
                  pl.BlockSpec((None, None, d, hid), lambda i, be, bs, nu: (layer, be[i], 0, 0)),
                  pl.BlockSpec((None, None, hid, d), lambda i, be, bs, nu: (layer, be[i], 0, 0))],
        out_specs=pl.BlockSpec((tr, d), lambda i, be, bs, nu: (i, 0)),
        scratch_shapes=[pltpu.VMEM((d, 2 * hid), BF16), pltpu.VMEM((hid, d), BF16)])
    return pl.pallas_call(
        kern,
        out_shape=jax.ShapeDtypeStruct((rows, d), F32),
        grid_spec=grid_spec,
        compiler_params=_cparams(1),
        name="routed_experts",
    )(block_e, block_src, n_used, xs, w_gate, w_up, w_down)


def _shared_kernel(u_ref, wg_ref, wu_ref, wd_ref, o_ref, wg_bf, wu_bf, wd_bf):
    @pl.when(pl.program_id(0) == 0)
    def _():
        wg_bf[...] = wg_ref[...].astype(BF16)
        wu_bf[...] = wu_ref[...].astype(BF16)
        wd_bf[...] = wd_ref[...].astype(BF16)

    xb = u_ref[...].astype(BF16)
    g = _dot(xb, wg_bf[...])
    h = (g * jax.nn.sigmoid(g)) * _dot(xb, wu_bf[...])
    o_ref[...] = _dot(h.astype(BF16), wd_bf[...])


def _shared_expert(u2, sh_gate, sh_up, sh_down, layer, tm=512):
    n, d = u2.shape
    hid = sh_gate.shape[2]
    return pl.pallas_call(
        _shared_kernel,
        out_shape=jax.ShapeDtypeStruct((n, d), F32),
        grid=(n // tm,),
        in_specs=[pl.BlockSpec((tm, d), lambda i: (i, 0)),
                  pl.BlockSpec((None, d, hid), lambda i: (layer, 0, 0)),
                  pl.BlockSpec((None, d, hid), lambda i: (layer, 0, 0)),
                  pl.BlockSpec((None, hid, d), lambda i: (layer, 0, 0))],
        out_specs=pl.BlockSpec((tm, d), lambda i: (i, 0)),
        scratch_shapes=[pltpu.VMEM((d, hid), BF16), pltpu.VMEM((d, hid), BF16), pltpu.VMEM((hid, d), BF16)],
        compiler_params=_cparams(1),
        name="shared_expert",
    )(u2, sh_gate, sh_up, sh_down)


def _combine_kernel(dest_ref, y_hbm, w8_ref, shd_ref, x_ref, gate_ref, g_ref, b_ref, sh_ref, sc_ref,
                    xo_ref, uo_ref, idx_smem, ybuf, idx_sem, row_sem, *, tm, n_tiles, alpha):
    i = pl.program_id(0)
    slot = i % 2

    def gather(tile, s):
        cp = pltpu.make_async_copy(dest_ref.at[tile], idx_smem.at[s], idx_sem)
        cp.start()
        cp.wait()

        def issue(g, carry):
            base = pl.multiple_of(g * 8, 8)
            for u in range(8):
                for kk in range(TOP_K):
                    src = idx_smem[s, kk, base + u]
                    pltpu.make_async_copy(y_hbm.at[pl.ds(src, 1)], ybuf.at[s, kk, pl.ds(base + u, 1)],
                                          row_sem.at[s]).start()
            return carry

        lax.fori_loop(0, tm // 8, issue, 0)

    @pl.when(i == 0)
    def _():
        gather(0, 0)

    @pl.when(i + 1 < n_tiles)
    def _():
        gather(i + 1, 1 - slot)

    for kk in range(TOP_K):
        pltpu.make_async_copy(y_hbm.at[pl.ds(0, tm)], ybuf.at[slot, kk], row_sem.at[slot]).wait()
    w8 = w8_ref[...]
    ffn = shd_ref[...]
    for kk in range(TOP_K):
        ffn = ffn + w8[:, kk:kk + 1] * ybuf[slot, kk]
    xn = _post_norm(x_ref[...], gate_ref[...], ffn, g_ref[...], b_ref[...], alpha)
    xo_ref[...] = xn
    uo_ref[...] = (xn * (1.0 + sc_ref[...]) + sh_ref[...]).astype(uo_ref.dtype)


def _combine_norm(dest_t, y_sorted, w8, shared, x2, mod_l, mod_next, ln_g, ln_b, layer, alpha, seq, tm=128):
    n, d = x2.shape
    n_tiles = n // tm
    tm = min(tm, seq)
    tps = seq // tm
    kern = functools.partial(_combine_kernel, tm=tm, n_tiles=n_tiles, alpha=alpha)
    row = lambda: pl.BlockSpec((tm, d), lambda i: (i, 0))
    vec = lambda: pl.BlockSpec((None, 1, d), lambda i: (layer, 0, 0))
    return pl.pallas_call(
        kern,
        out_shape=(jax.ShapeDtypeStruct((n, d), F32), jax.ShapeDtypeStruct((n, d), BF16)),
        grid=(n_tiles,),
        in_specs=[pl.BlockSpec((n_tiles, TOP_K, tm), lambda i: (0, 0, 0)),
                  pl.BlockSpec(memory_space=pl.ANY),
                  pl.BlockSpec((tm, TOP_K), lambda i: (i, 0)),
                  row(), row(), _mod_spec(5, d, tps), vec(), vec(),
                  _mod_spec(0, d, tps), _mod_spec(1, d, tps)],
        out_specs=(row(), row()),
        scratch_shapes=[pltpu.SMEM((2, TOP_K, tm), I32), pltpu.VMEM((2, TOP_K, tm, d), F32),
                        pltpu.SemaphoreType.DMA(()), pltpu.SemaphoreType.DMA((2,))],
        compiler_params=_cparams(1),
        name="combine_norm",
    )(dest_t.reshape(TOP_K, n_tiles, tm).transpose(1, 0, 2), y_sorted, w8, shared, x2, mod_l,
      ln_g.reshape(-1, 1, d), ln_b.reshape(-1, 1, d), mod_next, mod_next)


def _dispatch_plan(counts, tr, n_blocks):
    padded = (counts + tr - 1) // tr * tr
    end_padded = jnp.cumsum(padded)
    start_padded = (end_padded - padded).astype(I32)
    n_used = (end_padded[-1] // tr).astype(I32)
    block = jnp.arange(n_blocks, dtype=I32)
    block_src = jnp.minimum(block, n_used - 1)
    block_e = jnp.sum((block_src[:, None] * tr >= end_padded[None, :]).astype(I32), axis=1)
    return start_padded, block_e.astype(I32), block_src, n_used.reshape(1)


def kernel(x, c, ada_w, ada_b, w_in, gate_b, pool_w, pool_scale, conv_w, w_branch, w_out, ln1_g, ln1_b,
           router_w, router_bias, exp_w_gate, exp_w_up, exp_w_down, sh_w_gate, sh_w_up, sh_w_down, ln2_g, ln2_b):
    b, seq, d = x.shape
    depth = ada_w.shape[0]
    n = b * seq
    alpha = (2 * depth) ** 0.25
    expert_rows = 256
    n_blocks = -(-(n * TOP_K + N_EXPERTS * (expert_rows - 1)) // expert_rows)

    w = BRANCH_WIDTH
    cb = w // LANES
    sq, sk, sv = 0, cb, 2 * cb
    pool_col = 3 * w
    mq_col = 4 * w
    mq, mk, mv = mq_col // LANES, mq_col // LANES + cb, mq_col // LANES + 2 * cb
    conv_col = 7 * w
    gate_col = 10 * w

    mod = _ada(c, ada_w, ada_b)
    cos_t, sin_t = _rope_tables(seq)
    x2 = x.reshape(n, d)
    u = _modulate(x2, mod[0], seq)
    for l in range(depth):
        proj = _inproj(u, w_in, l, cos_t, sin_t, seq, rot_col=mq_col)
        proj3 = proj.reshape(b, seq, -1)
        y_sb = _sb_attention(proj3, sq, sk, sv).reshape(n, w)
        y_mb = _moba_attention(proj3, mq, mk, mv).reshape(n, w)
        y_pool, y_cv = _local_mixers(proj, pool_w[l], pool_scale[l], conv_w[l], seq,
                                     pool_col // w, conv_col // w, conv_col // w + 1, conv_col // w + 2)
        merged = _merge((y_sb, y_pool, y_mb, y_cv), proj, w_branch, gate_b[l], l, gate_col)
        x2, u2, logits = _outproj_norm(merged, w_out, l, x2, mod[l], ln1_g, ln1_b, router_w, alpha, seq)
        idx_t, w_t, pos_t, counts = _route(logits, router_bias, l)
        counts = counts[:, 0]
        start_padded, block_e, block_src, n_used = _dispatch_plan(counts, expert_rows, n_blocks)
        xs, dest_t = _dispatch(u2, idx_t, pos_t, counts, start_padded, n_used, expert_rows, n_blocks)
        y_sorted = _experts(xs, block_e, block_src, n_used, exp_w_gate, exp_w_up, exp_w_down, l, expert_rows)
        shared = _shared_expert(u2, sh_w_gate, sh_w_up, sh_w_down, l)
        x2, u = _combine_norm(dest_t, y_sorted, w_t.T, shared, x2, mod[l], mod[(l + 1) % depth],
                              ln2_g, ln2_b, l, alpha, seq)
    return x2.reshape(b, seq, d)
```

```python
import functools

import numpy as np
import jax
import jax.numpy as jnp
from jax import lax
from jax.experimental import pallas as pl
from jax.experimental.pallas import tpu as pltpu

F32 = jnp.float32
BF16 = jnp.bfloat16
I32 = jnp.int32
U32 = jnp.uint32

HEAD_DIM = 64
N_BRANCH = 4
BRANCH_WIDTH = 512
POOL_WINDOWS = (2, 4, 8, 16)
POOL_GROUP = 128
POOL_HALO = 16
CONV_K = 3
MOBA_BLOCK = 256
MOBA_TOPK = 3
ROPE_THETA = 10000.0
N_EXPERTS = 64
TOP_K = 8
EXPERT_HIDDEN = 384
ROUTED_SCALE = 2.5
LN_EPS = 1e-5
NEG_INF = float("-inf")
SB_DEAD_LOG = -100.0
MASKED = -1e30

LANES = 128
VMEM_LIMIT = 56 * 1024 * 1024


def _cparams(n_axes, vmem=VMEM_LIMIT):
    return pltpu.CompilerParams(dimension_semantics=("arbitrary",) * n_axes, vmem_limit_bytes=vmem)


def _dot(a, b):
    return jnp.dot(a, b, preferred_element_type=F32)


def _dot_nt(a, b):
    return lax.dot_general(a, b, (((1,), (1,)), ((), ())), preferred_element_type=F32)


def _split_bf16(x):
    hi = x.astype(BF16)
    lo = (x - hi.astype(F32)).astype(BF16)
    return hi, lo


def _ada_kernel(c_ref, w_ref, b_ref, o_ref):
    c = c_ref[...]
    cond = c * jax.nn.sigmoid(c)
    o_ref[...] = jnp.dot(cond, w_ref[...], preferred_element_type=F32,
                         precision=lax.Precision.HIGHEST) + b_ref[...]


def _ada(c, ada_w, ada_b):
    depth, d, d6 = ada_w.shape
    b = c.shape[0]
    rows = 8
    tn = 1024
    c8 = jnp.zeros((rows, d), F32).at[:b].set(c)
    out = pl.pallas_call(
        _ada_kernel,
        out_shape=jax.ShapeDtypeStruct((depth, rows, d6), F32),
        grid=(depth, d6 // tn),
        in_specs=[pl.BlockSpec((rows, d), lambda l, n: (0, 0)),
                  pl.BlockSpec((None, d, tn), lambda l, n: (l, 0, n)),
                  pl.BlockSpec((None, 1, tn), lambda l, n: (l, 0, n))],
        out_specs=pl.BlockSpec((None, rows, tn), lambda l, n: (l, 0, n)),
        compiler_params=_cparams(2),
        name="ada_mod",
    )(c8, ada_w, ada_b.reshape(depth, 1, d6))
    return out[:, :b].reshape(depth, b, 6, 1, d)


def _mod_spec(which, d, tiles_per_seq):
    return pl.BlockSpec((None, None, 1, d), lambda i, *_: (i // tiles_per_seq, which, 0, 0))


def _modulate_kernel(x_ref, sh_ref, sc_ref, u_ref):
    u_ref[...] = (x_ref[...] * (1.0 + sc_ref[...]) + sh_ref[...]).astype(u_ref.dtype)


def _modulate(x2, mod_l, seq, tm=512):
    n, d = x2.shape
    tm = min(tm, seq)
    tps = seq // tm
    return pl.pallas_call(
        _modulate_kernel,
        out_shape=jax.ShapeDtypeStruct((n, d), BF16),
        grid=(n // tm,),
        in_specs=[pl.BlockSpec((tm, d), lambda i: (i, 0)), _mod_spec(0, d, tps), _mod_spec(1, d, tps)],
        out_specs=pl.BlockSpec((tm, d), lambda i: (i, 0)),
        compiler_params=_cparams(1),
        name="modulate",
    )(x2, mod_l, mod_l)


def _inproj_kernel(u_ref, w_ref, cos_ref, sin_ref, o_ref, wbf_ref, *, rot_tile, tn):
    n = pl.program_id(0)
    m = pl.program_id(1)

    @pl.when(m == 0)
    def _():
        wbf_ref[...] = w_ref[...].astype(BF16)

    acc = _dot(u_ref[...], wbf_ref[...])

    @pl.when(n == rot_tile)
    def _():
        reps = tn // LANES
        cos = jnp.concatenate([cos_ref[...]] * reps, axis=1)
        sin = jnp.concatenate([sin_ref[...]] * reps, axis=1)
        lane = lax.broadcasted_iota(I32, acc.shape, 1)
        first_half = (lane % HEAD_DIM) < (HEAD_DIM // 2)
        half = HEAD_DIM // 2
        partner = jnp.where(first_half, pltpu.roll(acc, tn - half, 1), pltpu.roll(acc, half, 1))
        o_ref[...] = (acc * cos + partner * sin).astype(o_ref.dtype)

    @pl.when(n != rot_tile)
    def _():
        o_ref[...] = acc.astype(o_ref.dtype)


def _inproj(u, w_in, layer, cos_t, sin_t, seq, rot_col, tm=1024, tn=1024):
    n, d = u.shape
    cols = w_in.shape[2]
    tm = min(tm, seq)
    tps = seq // tm
    assert rot_col % tn == 0
    kern = functools.partial(_inproj_kernel, rot_tile=rot_col // tn, tn=tn)
    return pl.pallas_call(
        kern,
        out_shape=jax.ShapeDtypeStruct((n, cols), BF16),
        grid=(cols // tn, n // tm),
        in_specs=[pl.BlockSpec((tm, d), lambda j, i: (i, 0)),
                  pl.BlockSpec((None, d, tn), lambda j, i: (layer, 0, j)),
                  pl.BlockSpec((tm, LANES), lambda j, i: (i % tps, 0)),
                  pl.BlockSpec((tm, LANES), lambda j, i: (i % tps, 0))],
        out_specs=pl.BlockSpec((tm, tn), lambda j, i: (i, j)),
        scratch_shapes=[pltpu.VMEM((d, tn), BF16)],
        compiler_params=_cparams(2),
        name="in_proj",
    )(u, w_in, cos_t, sin_t)


def _rope_tables(seq):
    half = HEAD_DIM // 2
    inv_freq = ROPE_THETA ** (-jnp.arange(half, dtype=F32) / half)
    ang = jnp.arange(seq, dtype=F32)[:, None] * inv_freq[None, :]
    cos, sin = jnp.cos(ang), jnp.sin(ang)
    cos_h = jnp.concatenate([cos, cos], axis=1)
    sin_h = jnp.concatenate([-sin, sin], axis=1)
    reps = LANES // HEAD_DIM
    return jnp.tile(cos_h, (1, reps)), jnp.tile(sin_h, (1, reps))


def _sb_kernel(q_ref, k_ref, v_ref, o_ref, *, tq, tk):
    qi = pl.program_id(2)
    q0 = qi * tq
    r = lax.broadcasted_iota(I32, (tk, tk), 0)
    c = lax.broadcasted_iota(I32, (tk, tk), 1)
    later = (r > c).astype(BF16)
    heads = LANES // HEAD_DIM
    qs = [q_ref[:, h * HEAD_DIM:(h + 1) * HEAD_DIM] * (HEAD_DIM ** -0.5) for h in range(heads)]

    def block(j0, h, carry, acc, masked):
        k = k_ref[pl.ds(j0, tk), h * HEAD_DIM:(h + 1) * HEAD_DIM]
        v = v_ref[pl.ds(j0, tk), h * HEAD_DIM:(h + 1) * HEAD_DIM]
        z = _dot_nt(qs[h], k)
        softplus = jnp.maximum(z, 0.0) + jnp.log(1.0 + jnp.exp(-jnp.abs(z)))
        log_keep = -softplus
        if masked:
            qpos = q0 + lax.broadcasted_iota(I32, (tq, tk), 0)
            kpos = j0 + lax.broadcasted_iota(I32, (tq, tk), 1)
            past = kpos < qpos
            log_keep = jnp.where(past, log_keep, 0.0)
        hi, lo = _split_bf16(log_keep)
        between = _dot(hi, later) + _dot(lo, later) + carry
        w = jnp.exp((z - softplus) + between)
        if masked:
            w = jnp.where(past, w, 0.0)
        acc = acc + _dot(w.astype(BF16), v)
        carry = carry + jnp.sum(log_keep, axis=1, keepdims=True)
        return carry, acc

    state = [(jnp.zeros((tq, 1), F32), jnp.zeros((tq, HEAD_DIM), F32)) for _ in range(heads)]
    n_diag = tq // tk
    for d in range(n_diag - 1, -1, -1):
        j0 = pl.multiple_of(q0 + d * tk, tk)
        state = [block(j0, h, *state[h], True) for h in range(heads)]

    def live(pairs):
        return functools.reduce(jnp.maximum, [jnp.max(c) for c, _ in pairs])

    def cond(st):
        return jnp.logical_and(st[0] < qi * n_diag, st[1] > SB_DEAD_LOG)

    def body(st):
        jj = st[0]
        j0 = pl.multiple_of(q0 - (jj + 1) * tk, tk)
        new = [block(j0, h, st[2 + 2 * h], st[3 + 2 * h], False) for h in range(heads)]
        return (jj + 1, live(new)) + tuple(x for pair in new for x in pair)

    final = lax.while_loop(cond, body, (jnp.int32(0), live(state)) + tuple(x for pair in state for x in pair))
    o_ref[...] = jnp.concatenate([final[3 + 2 * h] for h in range(heads)], axis=1).astype(o_ref.dtype)


def _sb_attention(proj3, q_blk, k_blk, v_blk, tq=256, tk=128):
    b, seq, _ = proj3.shape
    pairs = BRANCH_WIDTH // LANES
    kern = functools.partial(_sb_kernel, tq=tq, tk=tk)
    return pl.pallas_call(
        kern,
        out_shape=jax.ShapeDtypeStruct((b, seq, BRANCH_WIDTH), BF16),
        grid=(b, pairs, seq // tq),
        in_specs=[pl.BlockSpec((None, tq, LANES), lambda bi, p, qi: (bi, qi, q_blk + p)),
                  pl.BlockSpec((None, seq, LANES), lambda bi, p, qi: (bi, 0, k_blk + p)),
                  pl.BlockSpec((None, seq, LANES), lambda bi, p, qi: (bi, 0, v_blk + p))],
        out_specs=pl.BlockSpec((None, tq, LANES), lambda bi, p, qi: (bi, qi, p)),
        compiler_params=_cparams(3),
        name="sb_attention",
    )(proj3, proj3, proj3)


def _moba_kernel(q_ref, k_ref, v_ref, o_ref, kmean_ref, kt_ref, vaug_ref, s_ref, *, n_kb):
    own = pl.program_id(2)
    tq = MOBA_BLOCK
    heads = LANES // HEAD_DIM
    kb_pad = -(-n_kb // 16) * 16
    spare = LANES - HEAD_DIM - kb_pad

    @pl.when(own == 0)
    def _():
        ones = jnp.ones((MOBA_BLOCK, HEAD_DIM), BF16)
        block_row = lax.broadcasted_iota(I32, (kb_pad, MOBA_BLOCK), 0)
        for jb in range(n_kb):
            rows = slice(jb * MOBA_BLOCK, (jb + 1) * MOBA_BLOCK)
            kb = k_ref[rows, :]
            kmean_ref[jb:jb + 1, :] = jnp.mean(kb.astype(F32), axis=0, keepdims=True)
            kt = kb.T
            tail = [(block_row == jb).astype(BF16), jnp.zeros((spare, MOBA_BLOCK), BF16)]
            vb = v_ref[rows, :]
            for h in range(heads):
                hs = slice(h * HEAD_DIM, (h + 1) * HEAD_DIM)
                kt_ref[jb, h] = jnp.concatenate([kt[hs, :]] + tail, axis=0)
                vaug_ref[h, rows, :] = jnp.concatenate([vb[:, hs], ones], axis=1)

    blk = lax.broadcasted_iota(I32, (n_kb, tq), 0)
    row = lax.broadcasted_iota(I32, (tq, MOBA_BLOCK), 0)
    col = lax.broadcasted_iota(I32, (tq, MOBA_BLOCK), 1)
    causal = col <= row

    qs, run_max = [], []
    for h in range(heads):
        hs = slice(h * HEAD_DIM, (h + 1) * HEAD_DIM)
        q = q_ref[:, hs] * (HEAD_DIM ** -0.5)
        km_hi, km_lo = _split_bf16(kmean_ref[:, hs])
        gate = _dot_nt(km_hi, q) + _dot_nt(km_lo, q)
        beaten = jnp.zeros((n_kb, tq), I32)
        for jp in range(n_kb):
            g = gate[jp:jp + 1, :]
            beats = (g > gate) | ((g == gate) & (jp < blk))
            beaten = beaten + jnp.where(beats, 1, 0) * (jp < own).astype(I32)
        visible = ((beaten < MOBA_TOPK) & (blk < own)) | (blk == own)
        bias = jnp.where(visible, 0.0, MASKED).T.astype(BF16)
        q_aug = jnp.concatenate([q, bias, jnp.zeros((tq, LANES - HEAD_DIM - n_kb), BF16)], axis=1)
        qs.append(q_aug)
        s = jnp.where(causal, _dot(q_aug, kt_ref[own, h]), NEG_INF)
        s_ref[h, own] = s
        run_max.append(s)

    def score_pass(j, running):
        out = []
        for h in range(heads):
            s = _dot(qs[h], kt_ref[j, h])
            s_ref[h, j] = s
            out.append(jnp.maximum(running[h], s))
        return tuple(out)

    run_max = lax.fori_loop(0, own, score_pass, tuple(run_max))
    row_max = [jnp.max(r, axis=1, keepdims=True) for r in run_max]

    def softmax_pass(j, accs):
        j0 = pl.multiple_of(j * MOBA_BLOCK, MOBA_BLOCK)
        out = []
        for h in range(heads):
            p = jnp.exp(s_ref[h, j] - row_max[h])
            out.append(accs[h] + _dot(p.astype(BF16), vaug_ref[h, pl.ds(j0, MOBA_BLOCK), :]))
        return tuple(out)

    accs = lax.fori_loop(0, own + 1, softmax_pass, tuple(jnp.zeros((tq, LANES), F32) for _ in range(heads)))
    o_ref[...] = jnp.concatenate([a[:, :HEAD_DIM] / a[:, HEAD_DIM:] for a in accs], axis=1).astype(o_ref.dtype)


def _moba_attention(proj3, q_blk, k_blk, v_blk):
    b, seq, _ = proj3.shape
    assert seq % MOBA_BLOCK == 0
    n_kb = seq // MOBA_BLOCK
    pairs = BRANCH_WIDTH // LANES
    kern = functools.partial(_moba_kernel, n_kb=n_kb)
    return pl.pallas_call(
        kern,
        out_shape=jax.ShapeDtypeStruct((b, seq, BRANCH_WIDTH), BF16),
        grid=(b, pairs, n_kb),
        in_specs=[pl.BlockSpec((None, MOBA_BLOCK, LANES), lambda bi, p, qi: (bi, qi, q_blk + p)),
                  pl.BlockSpec((None, seq, LANES), lambda bi, p, qi: (bi, 0, k_blk + p)),
                  pl.BlockSpec((None, seq, LANES), lambda bi, p, qi: (bi, 0, v_blk + p))],
        out_specs=pl.BlockSpec((None, MOBA_BLOCK, LANES), lambda bi, p, qi: (bi, qi, p)),
        scratch_shapes=[pltpu.VMEM((n_kb, LANES), F32),
                        pltpu.VMEM((n_kb, LANES // HEAD_DIM, LANES, MOBA_BLOCK), BF16),
                        pltpu.VMEM((LANES // HEAD_DIM, seq, LANES), BF16),
                        pltpu.VMEM((LANES // HEAD_DIM, n_kb, MOBA_BLOCK, MOBA_BLOCK), F32)],
        compiler_params=_cparams(3),
        name="moba_attention",
    )(proj3, proj3, proj3)


def _local_kernel(p_ref, ph_ref, ch_ref, chh_ref, cb_ref, cc_ref, cch_ref, pw_ref, ps_ref, cw_ref,
                  yp_ref, yc_ref, *, tm, tiles_per_seq):
    i = pl.program_id(0)
    seq_tile = i % tiles_per_seq
    has_left = seq_tile > 0

    def with_halo(halo_ref, body_ref):
        halo = jnp.where(has_left, halo_ref[...].astype(F32), 0.0)
        return jnp.concatenate([halo, body_ref[...].astype(F32)], axis=0)

    x = with_halo(ph_ref, p_ref)
    pos = seq_tile * tm + lax.broadcasted_iota(I32, (tm, 1), 0)
    for g, win in enumerate(POOL_WINDOWS):
        gs = slice(g * POOL_GROUP, (g + 1) * POOL_GROUP)
        xg = x[:, gs]
        s, k = xg, 1
        while k < win:
            s = s[k:] + s[:-k]
            k *= 2
        start = POOL_HALO + 1 - win
        window_sum = s[start:start + tm]
        count = jnp.minimum(pos + 1, win).astype(F32)
        mixed = window_sum / count - xg[POOL_HALO:]
        y = _dot(mixed.astype(BF16), pw_ref[g].astype(BF16)) * ps_ref[:, gs]
        yp_ref[:, gs] = y.astype(yp_ref.dtype)

    z = with_halo(chh_ref, ch_ref) * with_halo(cch_ref, cc_ref)
    y = jnp.zeros((tm, z.shape[1]), F32)
    for k in range(CONV_K):
        off = POOL_HALO - (CONV_K - 1) + k
        y = y + cw_ref[k:k + 1, :] * z[off:off + tm]
    yc_ref[...] = (cb_ref[...].astype(F32) * y).astype(yc_ref.dtype)


def _local_mixers(proj, pool_w, pool_scale, conv_w, seq, pool_blk, ch_blk, cb_blk, cc_blk, tm=512):
    n = proj.shape[0]
    w = BRANCH_WIDTH
    tm = min(tm, seq)
    tps = seq // tm
    hpt = tm // POOL_HALO

    def body(blk):
        return pl.BlockSpec((tm, w), lambda i: (i, blk))

    def halo(blk):
        return pl.BlockSpec((POOL_HALO, w), lambda i: (jnp.maximum(i * hpt - 1, 0), blk))

    kern = functools.partial(_local_kernel, tm=tm, tiles_per_seq=tps)
    return pl.pallas_call(
        kern,
        out_shape=(jax.ShapeDtypeStruct((n, w), BF16), jax.ShapeDtypeStruct((n, w), BF16)),
        grid=(n // tm,),
        in_specs=[body(pool_blk), halo(pool_blk), body(ch_blk), halo(ch_blk), body(cb_blk),
                  body(cc_blk), halo(cc_blk),
                  pl.BlockSpec(pool_w.shape, lambda i: (0, 0, 0)),
                  pl.BlockSpec((1, w), lambda i: (0, 0)),
                  pl.BlockSpec((CONV_K, w), lambda i: (0, 0))],
        out_specs=(pl.BlockSpec((tm, w), lambda i: (i, 0)), pl.BlockSpec((tm, w), lambda i: (i, 0))),
        compiler_params=_cparams(1),
        name="pool_conv",
    )(proj, proj, proj, proj, proj, proj, proj, pool_w, pool_scale.reshape(1, w), conv_w)


def _merge_kernel(b0, b1, b2, b3, g0, g1, g2, g3, wb_ref, gb_ref, o_ref, wbf_ref):
    @pl.when(pl.program_id(1) == 0)
    def _():
        wbf_ref[...] = wb_ref[...].astype(BF16)

    merged = None
    for nb, (br, gr) in enumerate(((b0, g0), (b1, g1), (b2, g2), (b3, g3))):
        lifted = _dot(br[...], wbf_ref[nb])
        gate = jax.nn.sigmoid(gr[...].astype(F32) + gb_ref[nb])
        term = gate * lifted
        merged = term if merged is None else merged + term
    o_ref[...] = merged.astype(o_ref.dtype)


def _merge(branches, proj, w_branch, gate_b, layer, gate_col, tm=1024, tn=512):
    n = proj.shape[0]
    d = w_branch.shape[3]
    assert gate_col % tn == 0 and d % tn == 0
    g_specs = [pl.BlockSpec((tm, tn), functools.partial(lambda j, i, nb: (i, (gate_col + nb * d) // tn + j), nb=nb))
               for nb in range(N_BRANCH)]
    b_specs = [pl.BlockSpec((tm, BRANCH_WIDTH), lambda j, i: (i, 0))] * N_BRANCH
    return pl.pallas_call(
        _merge_kernel,
        out_shape=jax.ShapeDtypeStruct((n, d), BF16),
        grid=(d // tn, n // tm),
        in_specs=b_specs + g_specs + [
            pl.BlockSpec((None, N_BRANCH, BRANCH_WIDTH, tn), lambda j, i: (layer, 0, 0, j)),
            pl.BlockSpec((N_BRANCH, 1, tn), lambda j, i: (0, 0, j))],
        out_specs=pl.BlockSpec((tm, tn), lambda j, i: (i, j)),
        scratch_shapes=[pltpu.VMEM((N_BRANCH, BRANCH_WIDTH, tn), BF16)],
        compiler_params=_cparams(2),
        name="gated_merge",
    )(*branches, proj, proj, proj, proj, w_branch, gate_b.reshape(N_BRANCH, 1, d))


def _post_norm(x, gate, y, g, b, alpha):
    h = alpha * x + (1.0 + gate) * y
    mu = jnp.mean(h, axis=1, keepdims=True)
    hc = h - mu
    var = jnp.mean(hc * hc, axis=1, keepdims=True)
    return hc * lax.rsqrt(var + LN_EPS) * g + b


def _outproj_kernel(mg_ref, w_ref, x_ref, gate_ref, g_ref, b_ref, sh_ref, sc_ref, rw_ref,
                    xo_ref, uo_ref, lg_ref, w_bf, *, alpha):
    @pl.when(pl.program_id(0) == 0)
    def _():
        w_bf[...] = w_ref[...].astype(BF16)

    y = _dot(mg_ref[...], w_bf[...])
    xn = _post_norm(x_ref[...], gate_ref[...], y, g_ref[...], b_ref[...], alpha)
    xo_ref[...] = xn
    u = xn * (1.0 + sc_ref[...]) + sh_ref[...]
    uo_ref[...] = u
    u_hi, u_lo = _split_bf16(u)
    w_hi, w_lo = _split_bf16(rw_ref[...])
    lg_ref[...] = _dot(u_hi, w_hi) + _dot(u_lo, w_hi) + _dot(u_hi, w_lo)


def _outproj_norm(merged, w_out, layer, x2, mod_l, ln_g, ln_b, router_w, alpha, seq, tm=256):
    n, d = x2.shape
    e = router_w.shape[2]
    tm = min(tm, seq)
    tps = seq // tm
    kern = functools.partial(_outproj_kernel, alpha=alpha)
    vec = lambda: pl.BlockSpec((None, 1, d), lambda i: (layer, 0, 0))
    row = lambda: pl.BlockSpec((tm, d), lambda i: (i, 0))
    once = pl.Buffered(1)
    return pl.pallas_call(
        kern,
        out_shape=(jax.ShapeDtypeStruct((n, d), F32), jax.ShapeDtypeStruct((n, d), F32),
                   jax.ShapeDtypeStruct((n, e), F32)),
        grid=(n // tm,),
        in_specs=[row(),
                  pl.BlockSpec((None, d, d), lambda i: (layer, 0, 0), pipeline_mode=once),
                  row(),
                  _mod_spec(2, d, tps), vec(), vec(), _mod_spec(3, d, tps), _mod_spec(4, d, tps),
                  pl.BlockSpec((None, d, e), lambda i: (layer, 0, 0), pipeline_mode=once)],
        out_specs=(row(), row(), pl.BlockSpec((tm, e), lambda i: (i, 0))),
        scratch_shapes=[pltpu.VMEM((d, d), BF16)],
        compiler_params=_cparams(1),
        name="out_proj_norm",
    )(merged, w_out, x2, mod_l, ln_g.reshape(-1, 1, d), ln_b.reshape(-1, 1, d), mod_l, mod_l, router_w)


def _route_kernel(lg_ref, bias_ref, idx_ref, w_ref, pos_ref, cnt_ref, carry_ref, *, tm):
    i = pl.program_id(0)

    @pl.when(i == 0)
    def _():
        carry_ref[...] = jnp.zeros_like(carry_ref)

    aff = jax.nn.sigmoid(lg_ref[...].T)
    e = aff.shape[0]
    sub = lax.broadcasted_iota(I32, (e, tm), 0).astype(F32)
    work = aff + bias_ref[...]
    picks = []
    chosen_f = jnp.zeros((e, tm), F32)
    for _ in range(TOP_K):
        best = jnp.max(work, axis=0, keepdims=True)
        pick = jnp.min(jnp.where(work == best, sub, float(e)), axis=0, keepdims=True)
        onehot = sub == pick
        picks.append((pick, onehot))
        chosen_f = jnp.where(onehot, 1.0, chosen_f)
        work = jnp.where(onehot, NEG_INF, work)

    sel_aff = chosen_f * aff
    dense_w = sel_aff / jnp.sum(sel_aff, axis=0, keepdims=True) * ROUTED_SCALE
    r = lax.broadcasted_iota(I32, (tm, tm), 0)
    c = lax.broadcasted_iota(I32, (tm, tm), 1)
    earlier = (r < c).astype(BF16)
    rank = _dot(chosen_f.astype(BF16), earlier) + carry_ref[...]
    carry_ref[...] += jnp.sum(chosen_f, axis=1, keepdims=True)
    cnt_ref[...] = carry_ref[...].astype(I32)

    slot = lax.broadcasted_iota(I32, (TOP_K, tm), 0)
    idx8 = jnp.zeros((TOP_K, tm), F32)
    w8 = jnp.zeros((TOP_K, tm), F32)
    pos8 = jnp.zeros((TOP_K, tm), F32)
    for kk, (pick, onehot) in enumerate(picks):
        idx8 = jnp.where(slot == kk, pick, idx8)
        w8 = jnp.where(slot == kk, jnp.sum(jnp.where(onehot, dense_w, 0.0), axis=0, keepdims=True), w8)
        pos8 = jnp.where(slot == kk, jnp.sum(jnp.where(onehot, rank, 0.0), axis=0, keepdims=True), pos8)
    idx_ref[...] = idx8.astype(I32)
    w_ref[...] = w8
    pos_ref[...] = pos8.astype(I32)


def _route(logits, router_bias, layer, tm=512):
    n, e = logits.shape
    kern = functools.partial(_route_kernel, tm=tm)
    slots = lambda: pl.BlockSpec((TOP_K, tm), lambda i: (0, i))
    return pl.pallas_call(
        kern,
        out_shape=(jax.ShapeDtypeStruct((TOP_K, n), I32), jax.ShapeDtypeStruct((TOP_K, n), F32),
                   jax.ShapeDtypeStruct((TOP_K, n), I32), jax.ShapeDtypeStruct((e, 1), I32)),
        grid=(n // tm,),
        in_specs=[pl.BlockSpec((tm, e), lambda i: (i, 0)),
                  pl.BlockSpec((None, e, 1), lambda i: (layer, 0, 0))],
        out_specs=(slots(), slots(), slots(), pl.BlockSpec((e, 1), lambda i: (0, 0))),
        scratch_shapes=[pltpu.VMEM((e, 1), F32)],
        compiler_params=_cparams(1),
        name="route",
    )(logits, router_bias.reshape(-1, e, 1))


def _dispatch_kernel(cnt_ref, start_ref, nu_ref, u_ref, idx_ref, pos_ref, dest_ref, xs_hbm,
                     dest_smem, zrow_ref, zblk_ref, idx_sem, row_sem, zero_sem,
                     *, tm, tr, n_tiles, n_blocks, n_experts):
    i = pl.program_id(0)
    idx = idx_ref[...]
    dest = pos_ref[...]
    for e in range(n_experts):
        dest = dest + jnp.where(idx == e, start_ref[e], 0)
    dest_ref[...] = dest
    idx_cp = pltpu.make_async_copy(dest_ref, dest_smem, idx_sem)
    idx_cp.start()
    idx_cp.wait()

    def issue(g, carry):
        base = pl.multiple_of(g * 8, 8)
        for u in range(8):
            for kk in range(TOP_K):
                dst = dest_smem[kk, base + u]
                pltpu.make_async_copy(u_ref.at[pl.ds(base + u, 1)], xs_hbm.at[pl.ds(dst, 1)],
                                      row_sem).start(priority=kk % 2)
        return carry

    lax.fori_loop(0, tm // 8, issue, 0)
    for kk in range(TOP_K):
        pltpu.make_async_copy(u_ref, xs_hbm.at[pl.ds(0, tm)], row_sem).wait()

    @pl.when(i == n_tiles - 1)
    def _():
        zrow_ref[...] = jnp.zeros_like(zrow_ref)
        zblk_ref[...] = jnp.zeros_like(zblk_ref)

        def pad_expert(e, carry):
            cnt = cnt_ref[e]
            first = start_ref[e] + cnt
            n_pad = (tr - cnt % tr) % tr

            def start(r, c):
                pltpu.make_async_copy(zrow_ref.at[pl.ds(0, 1)], xs_hbm.at[pl.ds(first + r, 1)], zero_sem).start()
                return c

            def wait(r, c):
                pltpu.make_async_copy(zrow_ref.at[pl.ds(0, 1)], xs_hbm.at[pl.ds(0, 1)], zero_sem).wait()
                return c

            lax.fori_loop(0, n_pad, start, 0)
            lax.fori_loop(0, n_pad, wait, 0)
            return carry

        lax.fori_loop(0, n_experts, pad_expert, 0)

        def tail_start(blk, c):
            row0 = pl.multiple_of(blk * tr, tr)
            pltpu.make_async_copy(zblk_ref, xs_hbm.at[pl.ds(row0, tr)], zero_sem).start()
            return c

        def tail_wait(blk, c):
            pltpu.make_async_copy(zblk_ref, xs_hbm.at[pl.ds(0, tr)], zero_sem).wait()
            return c

        lax.fori_loop(nu_ref[0], n_blocks, tail_start, 0)
        lax.fori_loop(nu_ref[0], n_blocks, tail_wait, 0)


def _dispatch(u2, idx_t, pos_t, counts, start_padded, n_used, tr, n_blocks, tm=512):
    n, d = u2.shape
    n_tiles = n // tm
    kern = functools.partial(_dispatch_kernel, tm=tm, tr=tr, n_tiles=n_tiles, n_blocks=n_blocks,
                             n_experts=counts.shape[0])
    slots = lambda: pl.BlockSpec((TOP_K, tm), lambda i, *_: (0, i))
    grid_spec = pltpu.PrefetchScalarGridSpec(
        num_scalar_prefetch=3,
        grid=(n_tiles,),
        in_specs=[pl.BlockSpec((tm, d), lambda i, *_: (i, 0)), slots(), slots()],
        out_specs=(slots(), pl.BlockSpec(memory_space=pl.ANY)),
        scratch_shapes=[pltpu.SMEM((TOP_K, tm), I32), pltpu.VMEM((8, d), F32), pltpu.VMEM((tr, d), F32),
                        pltpu.SemaphoreType.DMA(()), pltpu.SemaphoreType.DMA(()), pltpu.SemaphoreType.DMA(())])
    dest_t, xs = pl.pallas_call(
        kern,
        out_shape=(jax.ShapeDtypeStruct((TOP_K, n), I32), jax.ShapeDtypeStruct((n_blocks * tr, d), F32)),
        grid_spec=grid_spec,
        compiler_params=_cparams(1),
        name="dispatch",
    )(counts, start_padded, n_used, u2, idx_t, pos_t)
    return xs, dest_t


def _experts_kernel(be_ref, bs_ref, nxt_ref, nu_ref, x_ref, wg_hbm, wu_hbm, wd_hbm, y_ref,
                    wg_buf, wu_buf, wd_buf, wgu_bf, wd_bf, slot_ref, sems, *, hid, layer):
    i = pl.program_id(0)

    def fetch(e, s):
        return (pltpu.make_async_copy(wg_hbm.at[layer, e], wg_buf.at[s], sems.at[s, 0]),
                pltpu.make_async_copy(wu_hbm.at[layer, e], wu_buf.at[s], sems.at[s, 1]),
                pltpu.make_async_copy(wd_hbm.at[layer, e], wd_buf.at[s], sems.at[s, 2]))

    @pl.when(i == 0)
    def _():
        slot_ref[0] = 0
        for cp in fetch(be_ref[0], 0):
            cp.start()

    @pl.when(i < nu_ref[0])
    def _():
        e = be_ref[i]
        changed = jnp.logical_or(i == 0, e != be_ref[jnp.maximum(i - 1, 0)])

        @pl.when(changed)
        def _():
            s = slot_ref[0]
            for cp in fetch(e, s):
                cp.wait()
            nxt = nxt_ref[e]

            @pl.when(nxt >= 0)
            def _():
                for cp in fetch(nxt, 1 - s):
                    cp.start()

            wgu_bf[:, :hid] = wg_buf[s].astype(BF16)
            wgu_bf[:, hid:] = wu_buf[s].astype(BF16)
            wd_bf[...] = wd_buf[s].astype(BF16)
            slot_ref[0] = 1 - s

        gu = _dot(x_ref[...].astype(BF16), wgu_bf[...])
        g = gu[:, :hid]
        h = (g * jax.nn.sigmoid(g)) * gu[:, hid:]
        y_ref[...] = _dot(h.astype(BF16), wd_bf[...])

    @pl.when(i >= nu_ref[0])
    def _():
        y_ref[...] = jnp.zeros_like(y_ref)


def _experts(xs, block_e, block_src, next_e, n_used, w_gate, w_up, w_down, layer, tr):
    rows, d = xs.shape
    n_blocks = rows // tr
    hid = w_gate.shape[3]
    kern = functools.partial(_experts_kernel, hid=hid, layer=layer)
    hbm = lambda: pl.BlockSpec(memory_space=pl.ANY)
    grid_spec = pltpu.PrefetchScalarGridSpec(
        num_scalar_prefetch=4,
        grid=(n_blocks,),
        in_specs=[pl.BlockSpec((tr, d), lambda i, be, bs, nx, nu: (bs[i], 0)), hbm(), hbm(), hbm()],
        out_specs=pl.BlockSpec((tr, d), lambda i, be, bs, nx, nu: (i, 0)),
        scratch_shapes=[pltpu.VMEM((2, d, hid), F32), pltpu.VMEM((2, d, hid), F32), pltpu.VMEM((2, hid, d), F32),
                        pltpu.VMEM((d, 2 * hid), BF16), pltpu.VMEM((hid, d), BF16),
                        pltpu.SMEM((1,), I32), pltpu.SemaphoreType.DMA((2, 3))])
    return pl.pallas_call(
        kern,
        out_shape=jax.ShapeDtypeStruct((rows, d), F32),
        grid_spec=grid_spec,
        compiler_params=_cparams(1),
        name="routed_experts",
    )(block_e, block_src, next_e, n_used, xs, w_gate, w_up, w_down)


def _shared_kernel(u_ref, wg_ref, wu_ref, wd_ref, o_ref, wg_bf, wu_bf, wd_bf):
    @pl.when(pl.program_id(0) == 0)
    def _():
        wg_bf[...] = wg_ref[...].astype(BF16)
        wu_bf[...] = wu_ref[...].astype(BF16)
        wd_bf[...] = wd_ref[...].astype(BF16)

    xb = u_ref[...].astype(BF16)
    g = _dot(xb, wg_bf[...])
    h = (g * jax.nn.sigmoid(g)) * _dot(xb, wu_bf[...])
    o_ref[...] = _dot(h.astype(BF16), wd_bf[...])


def _shared_expert(u2, sh_gate, sh_up, sh_down, layer, tm=512):
    n, d = u2.shape
    hid = sh_gate.shape[2]
    return pl.pallas_call(
        _shared_kernel,
        out_shape=jax.ShapeDtypeStruct((n, d), F32),
        grid=(n // tm,),
        in_specs=[pl.BlockSpec((tm, d), lambda i: (i, 0)),
                  pl.BlockSpec((None, d, hid), lambda i: (layer, 0, 0)),
                  pl.BlockSpec((None, d, hid), lambda i: (layer, 0, 0)),
                  pl.BlockSpec((None, hid, d), lambda i: (layer, 0, 0))],
        out_specs=pl.BlockSpec((tm, d), lambda i: (i, 0)),
        scratch_shapes=[pltpu.VMEM((d, hid), BF16), pltpu.VMEM((d, hid), BF16), pltpu.VMEM((hid, d), BF16)],
        compiler_params=_cparams(1),
        name="shared_expert",
    )(u2, sh_gate, sh_up, sh_down)


def _combine_kernel(dest_ref, y_hbm, w8_ref, shd_ref, x_ref, gate_ref, g_ref, b_ref, sh_ref, sc_ref,
                    xo_ref, uo_ref, idx_smem, ybuf, idx_sem, row_sem, *, tm, n_tiles, alpha):
    i = pl.program_id(0)
    slot = i % 2

    def gather(tile, s):
        cp = pltpu.make_async_copy(dest_ref.at[tile], idx_smem.at[s], idx_sem)
        cp.start()
        cp.wait()

        def issue(g, carry):
            base = pl.multiple_of(g * 8, 8)
            for u in range(8):
                for kk in range(TOP_K):
                    src = idx_smem[s, kk, base + u]
                    pltpu.make_async_copy(y_hbm.at[pl.ds(src, 1)], ybuf.at[s, kk, pl.ds(base + u, 1)],
                                          row_sem.at[s]).start(priority=kk % 2)
            return carry

        lax.fori_loop(0, tm // 8, issue, 0)

    @pl.when(i == 0)
    def _():
        gather(0, 0)

    @pl.when(i + 1 < n_tiles)
    def _():
        gather(i + 1, 1 - slot)

    for kk in range(TOP_K):
        pltpu.make_async_copy(y_hbm.at[pl.ds(0, tm)], ybuf.at[slot, kk], row_sem.at[slot]).wait()
    w8 = w8_ref[...]
    ffn = shd_ref[...]
    for kk in range(TOP_K):
        ffn = ffn + w8[:, kk:kk + 1] * ybuf[slot, kk]
    xn = _post_norm(x_ref[...], gate_ref[...], ffn, g_ref[...], b_ref[...], alpha)
    xo_ref[...] = xn
    uo_ref[...] = (xn * (1.0 + sc_ref[...]) + sh_ref[...]).astype(uo_ref.dtype)


def _combine_norm(dest_t, y_sorted, w8, shared, x2, mod_l, mod_next, ln_g, ln_b, layer, alpha, seq, tm=128):
    n, d = x2.shape
    n_tiles = n // tm
    tm = min(tm, seq)
    tps = seq // tm
    kern = functools.partial(_combine_kernel, tm=tm, n_tiles=n_tiles, alpha=alpha)
    row = lambda: pl.BlockSpec((tm, d), lambda i: (i, 0))
    vec = lambda: pl.BlockSpec((None, 1, d), lambda i: (layer, 0, 0))
    return pl.pallas_call(
        kern,
        out_shape=(jax.ShapeDtypeStruct((n, d), F32), jax.ShapeDtypeStruct((n, d), BF16)),
        grid=(n_tiles,),
        in_specs=[pl.BlockSpec((n_tiles, TOP_K, tm), lambda i: (0, 0, 0)),
                  pl.BlockSpec(memory_space=pl.ANY),
                  pl.BlockSpec((tm, TOP_K), lambda i: (i, 0)),
                  row(), row(), _mod_spec(5, d, tps), vec(), vec(),
                  _mod_spec(0, d, tps), _mod_spec(1, d, tps)],
        out_specs=(row(), row()),
        scratch_shapes=[pltpu.SMEM((2, TOP_K, tm), I32), pltpu.VMEM((2, TOP_K, tm, d), F32),
                        pltpu.SemaphoreType.DMA(()), pltpu.SemaphoreType.DMA((2,))],
        compiler_params=_cparams(1),
        name="combine_norm",
    )(dest_t.reshape(TOP_K, n_tiles, tm).transpose(1, 0, 2), y_sorted, w8, shared, x2, mod_l,
      ln_g.reshape(-1, 1, d), ln_b.reshape(-1, 1, d), mod_next, mod_next)


def _dispatch_plan(counts, tr, n_blocks):
    padded = (counts + tr - 1) // tr * tr
    end_padded = jnp.cumsum(padded)
    start_padded = (end_padded - padded).astype(I32)
    n_used = (end_padded[-1] // tr).astype(I32)
    block = jnp.arange(n_blocks, dtype=I32)
    block_src = jnp.minimum(block, n_used - 1)
    block_e = jnp.sum((block_src[:, None] * tr >= end_padded[None, :]).astype(I32), axis=1)
    n_e = counts.shape[0]
    ids = jnp.where(counts > 0, jnp.arange(n_e, dtype=I32), n_e)
    later = jnp.concatenate([lax.cummin(ids[::-1])[::-1][1:], jnp.full((1,), n_e, I32)])
    next_e = jnp.where(later < n_e, later, -1).astype(I32)
    return start_padded, block_e.astype(I32), block_src, next_e, n_used.reshape(1)


def kernel(x, c, ada_w, ada_b, w_in, gate_b, pool_w, pool_scale, conv_w, w_branch, w_out, ln1_g, ln1_b,
           router_w, router_bias, exp_w_gate, exp_w_up, exp_w_down, sh_w_gate, sh_w_up, sh_w_down, ln2_g, ln2_b):
    b, seq, d = x.shape
    depth = ada_w.shape[0]
    n = b * seq
    alpha = (2 * depth) ** 0.25
    expert_rows = 256
    n_blocks = -(-(n * TOP_K + N_EXPERTS * (expert_rows - 1)) // expert_rows)

    w = BRANCH_WIDTH
    cb = w // LANES
    sq, sk, sv = 0, cb, 2 * cb
    pool_col = 3 * w
    mq_col = 4 * w
    mq, mk, mv = mq_col // LANES, mq_col // LANES + cb, mq_col // LANES + 2 * cb
    conv_col = 7 * w
    gate_col = 10 * w

    mod = _ada(c, ada_w, ada_b)
    cos_t, sin_t = _rope_tables(seq)
    x2 = x.reshape(n, d)
    u = _modulate(x2, mod[0], seq)
    for l in range(depth):
        proj = _inproj(u, w_in, l, cos_t, sin_t, seq, rot_col=mq_col)
        proj3 = proj.reshape(b, seq, -1)
        y_sb = _sb_attention(proj3, sq, sk, sv).reshape(n, w)
        y_mb = _moba_attention(proj3, mq, mk, mv).reshape(n, w)
        y_pool, y_cv = _local_mixers(proj, pool_w[l], pool_scale[l], conv_w[l], seq,
                                     pool_col // w, conv_col // w, conv_col // w + 1, conv_col // w + 2)
        merged = _merge((y_sb, y_pool, y_mb, y_cv), proj, w_branch, gate_b[l], l, gate_col)
        x2, u2, logits = _outproj_norm(merged, w_out, l, x2, mod[l], ln1_g, ln1_b, router_w, alpha, seq)
        idx_t, w_t, pos_t, counts = _route(logits, router_bias, l)
        counts = counts[:, 0]
        start_padded, block_e, block_src, next_e, n_used = _dispatch_plan(counts, expert_rows, n_blocks)
        xs, dest_t = _dispatch(u2, idx_t, pos_t, counts, start_padded, n_used, expert_rows, n_blocks)
        y_sorted = _experts(xs, block_e, block_src, next_e, n_used, exp_w_gate, exp_w_up, exp_w_down, l,
                            expert_rows)
        shared = _shared_expert(u2, sh_w_gate, sh_w_up, sh_w_down, l)
        x2, u = _combine_norm(dest_t, y_sorted, w_t.T, shared, x2, mod[l], mod[(l + 1) % depth],
                              ln2_g, ln2_b, l, alpha, seq)
    return x2.reshape(b, seq, d)
```

```python
import functools

import numpy as np
import jax
import jax.numpy as jnp
from jax import lax
from jax.experimental import pallas as pl
from jax.experimental.pallas import tpu as pltpu

F32 = jnp.float32
BF16 = jnp.bfloat16
I32 = jnp.int32
U32 = jnp.uint32

HEAD_DIM = 64
N_BRANCH = 4
BRANCH_WIDTH = 512
POOL_WINDOWS = (2, 4, 8, 16)
POOL_GROUP = 128
POOL_HALO = 16
CONV_K = 3
MOBA_BLOCK = 256
MOBA_TOPK = 3
ROPE_THETA = 10000.0
N_EXPERTS = 64
TOP_K = 8
EXPERT_HIDDEN = 384
ROUTED_SCALE = 2.5
LN_EPS = 1e-5
NEG_INF = float("-inf")
SB_DEAD_LOG = -100.0
MASKED = -1e30

LANES = 128
VMEM_LIMIT = 56 * 1024 * 1024


def _cparams(n_axes, vmem=VMEM_LIMIT):
    return pltpu.CompilerParams(dimension_semantics=("arbitrary",) * n_axes, vmem_limit_bytes=vmem)


def _dot(a, b):
    return jnp.dot(a, b, preferred_element_type=F32)


def _dot_nt(a, b):
    return lax.dot_general(a, b, (((1,), (1,)), ((), ())), preferred_element_type=F32)


def _split_bf16(x):
    hi = x.astype(BF16)
    lo = (x - hi.astype(F32)).astype(BF16)
    return hi, lo


def _ada_kernel(c_ref, w_ref, b_ref, o_ref):
    c = c_ref[...]
    cond = c * jax.nn.sigmoid(c)
    o_ref[...] = jnp.dot(cond, w_ref[...], preferred_element_type=F32,
                         precision=lax.Precision.HIGHEST) + b_ref[...]


def _ada(c, ada_w, ada_b):
    depth, d, d6 = ada_w.shape
    b = c.shape[0]
    rows = 8
    tn = 1024
    c8 = jnp.zeros((rows, d), F32).at[:b].set(c)
    out = pl.pallas_call(
        _ada_kernel,
        out_shape=jax.ShapeDtypeStruct((depth, rows, d6), F32),
        grid=(depth, d6 // tn),
        in_specs=[pl.BlockSpec((rows, d), lambda l, n: (0, 0)),
                  pl.BlockSpec((None, d, tn), lambda l, n: (l, 0, n)),
                  pl.BlockSpec((None, 1, tn), lambda l, n: (l, 0, n))],
        out_specs=pl.BlockSpec((None, rows, tn), lambda l, n: (l, 0, n)),
        compiler_params=_cparams(2),
        name="ada_mod",
    )(c8, ada_w, ada_b.reshape(depth, 1, d6))
    return out[:, :b].reshape(depth, b, 6, 1, d)


def _mod_spec(which, d, tiles_per_seq):
    return pl.BlockSpec((None, None, 1, d), lambda i, *_: (i // tiles_per_seq, which, 0, 0))


def _modulate_kernel(x_ref, sh_ref, sc_ref, u_ref):
    u_ref[...] = (x_ref[...] * (1.0 + sc_ref[...]) + sh_ref[...]).astype(u_ref.dtype)


def _modulate(x2, mod_l, seq, tm=512):
    n, d = x2.shape
    tm = min(tm, seq)
    tps = seq // tm
    return pl.pallas_call(
        _modulate_kernel,
        out_shape=jax.ShapeDtypeStruct((n, d), BF16),
        grid=(n // tm,),
        in_specs=[pl.BlockSpec((tm, d), lambda i: (i, 0)), _mod_spec(0, d, tps), _mod_spec(1, d, tps)],
        out_specs=pl.BlockSpec((tm, d), lambda i: (i, 0)),
        compiler_params=_cparams(1),
        name="modulate",
    )(x2, mod_l, mod_l)


def _inproj_kernel(u_ref, w_ref, cos_ref, sin_ref, o_ref, wbf_ref, *, rot_tile, tn):
    n = pl.program_id(0)
    m = pl.program_id(1)

    @pl.when(m == 0)
    def _():
        wbf_ref[...] = w_ref[...].astype(BF16)

    acc = _dot(u_ref[...], wbf_ref[...])

    @pl.when(n == rot_tile)
    def _():
        reps = tn // LANES
        cos = jnp.concatenate([cos_ref[...]] * reps, axis=1)
        sin = jnp.concatenate([sin_ref[...]] * reps, axis=1)
        lane = lax.broadcasted_iota(I32, acc.shape, 1)
        first_half = (lane % HEAD_DIM) < (HEAD_DIM // 2)
        half = HEAD_DIM // 2
        partner = jnp.where(first_half, pltpu.roll(acc, tn - half, 1), pltpu.roll(acc, half, 1))
        o_ref[...] = (acc * cos + partner * sin).astype(o_ref.dtype)

    @pl.when(n != rot_tile)
    def _():
        o_ref[...] = acc.astype(o_ref.dtype)


def _inproj(u, w_in, layer, cos_t, sin_t, seq, rot_col, tm=1024, tn=1024):
    n, d = u.shape
    cols = w_in.shape[2]
    tm = min(tm, seq)
    tps = seq // tm
    assert rot_col % tn == 0
    kern = functools.partial(_inproj_kernel, rot_tile=rot_col // tn, tn=tn)
    return pl.pallas_call(
        kern,
        out_shape=jax.ShapeDtypeStruct((n, cols), BF16),
        grid=(cols // tn, n // tm),
        in_specs=[pl.BlockSpec((tm, d), lambda j, i: (i, 0)),
                  pl.BlockSpec((None, d, tn), lambda j, i: (layer, 0, j)),
                  pl.BlockSpec((tm, LANES), lambda j, i: (i % tps, 0)),
                  pl.BlockSpec((tm, LANES), lambda j, i: (i % tps, 0))],
        out_specs=pl.BlockSpec((tm, tn), lambda j, i: (i, j)),
        scratch_shapes=[pltpu.VMEM((d, tn), BF16)],
        compiler_params=_cparams(2),
        name="in_proj",
    )(u, w_in, cos_t, sin_t)


def _rope_tables(seq):
    half = HEAD_DIM // 2
    inv_freq = ROPE_THETA ** (-jnp.arange(half, dtype=F32) / half)
    ang = jnp.arange(seq, dtype=F32)[:, None] * inv_freq[None, :]
    cos, sin = jnp.cos(ang), jnp.sin(ang)
    cos_h = jnp.concatenate([cos, cos], axis=1)
    sin_h = jnp.concatenate([-sin, sin], axis=1)
    reps = LANES // HEAD_DIM
    return jnp.tile(cos_h, (1, reps)), jnp.tile(sin_h, (1, reps))


def _sb_kernel(q_ref, k_ref, v_ref, o_ref, *, tq, tk):
    qi = pl.program_id(2)
    q0 = qi * tq
    r = lax.broadcasted_iota(I32, (tk, tk), 0)
    c = lax.broadcasted_iota(I32, (tk, tk), 1)
    later = (r > c).astype(BF16)
    heads = LANES // HEAD_DIM
    qs = [q_ref[:, h * HEAD_DIM:(h + 1) * HEAD_DIM] * (HEAD_DIM ** -0.5) for h in range(heads)]

    def block(j0, h, carry, acc, masked):
        k = k_ref[pl.ds(j0, tk), h * HEAD_DIM:(h + 1) * HEAD_DIM]
        v = v_ref[pl.ds(j0, tk), h * HEAD_DIM:(h + 1) * HEAD_DIM]
        z = _dot_nt(qs[h], k)
        softplus = jnp.maximum(z, 0.0) + jnp.log(1.0 + jnp.exp(-jnp.abs(z)))
        log_keep = -softplus
        if masked:
            qpos = q0 + lax.broadcasted_iota(I32, (tq, tk), 0)
            kpos = j0 + lax.broadcasted_iota(I32, (tq, tk), 1)
            past = kpos < qpos
            log_keep = jnp.where(past, log_keep, 0.0)
        hi, lo = _split_bf16(log_keep)
        between = _dot(hi, later) + _dot(lo, later) + carry
        w = jnp.exp((z - softplus) + between)
        if masked:
            w = jnp.where(past, w, 0.0)
        acc = acc + _dot(w.astype(BF16), v)
        carry = carry + jnp.sum(log_keep, axis=1, keepdims=True)
        return carry, acc

    state = [(jnp.zeros((tq, 1), F32), jnp.zeros((tq, HEAD_DIM), F32)) for _ in range(heads)]
    n_diag = tq // tk
    for d in range(n_diag - 1, -1, -1):
        j0 = pl.multiple_of(q0 + d * tk, tk)
        state = [block(j0, h, *state[h], True) for h in range(heads)]

    def live(pairs):
        return functools.reduce(jnp.maximum, [jnp.max(c) for c, _ in pairs])

    def cond(st):
        return jnp.logical_and(st[0] < qi, st[1] > SB_DEAD_LOG)

    def body(st):
        jj = st[0]
        new = [(st[2 + 2 * h], st[3 + 2 * h]) for h in range(heads)]
        for part in range(n_diag):
            j0 = pl.multiple_of(q0 - (jj * n_diag + part + 1) * tk, tk)
            new = [block(j0, h, *new[h], False) for h in range(heads)]
        return (jj + 1, live(new)) + tuple(x for pair in new for x in pair)

    final = lax.while_loop(cond, body, (jnp.int32(0), live(state)) + tuple(x for pair in state for x in pair))
    o_ref[...] = jnp.concatenate([final[3 + 2 * h] for h in range(heads)], axis=1).astype(o_ref.dtype)


def _sb_attention(proj3, q_blk, k_blk, v_blk, tq=256, tk=128):
    b, seq, _ = proj3.shape
    pairs = BRANCH_WIDTH // LANES
    kern = functools.partial(_sb_kernel, tq=tq, tk=tk)
    return pl.pallas_call(
        kern,
        out_shape=jax.ShapeDtypeStruct((b, seq, BRANCH_WIDTH), BF16),
        grid=(b, pairs, seq // tq),
        in_specs=[pl.BlockSpec((None, tq, LANES), lambda bi, p, qi: (bi, qi, q_blk + p)),
                  pl.BlockSpec((None, seq, LANES), lambda bi, p, qi: (bi, 0, k_blk + p)),
                  pl.BlockSpec((None, seq, LANES), lambda bi, p, qi: (bi, 0, v_blk + p))],
        out_specs=pl.BlockSpec((None, tq, LANES), lambda bi, p, qi: (bi, qi, p)),
        compiler_params=_cparams(3),
        name="sb_attention",
    )(proj3, proj3, proj3)


def _moba_kernel(q_ref, k_ref, v_ref, o_ref, kmean_ref, kt_ref, vaug_ref, s_ref, *, n_kb):
    own = pl.program_id(2)
    tq = MOBA_BLOCK
    heads = LANES // HEAD_DIM
    last = n_kb - 1

    @pl.when(own == 0)
    def _():
        s_ref[...] = jnp.zeros_like(s_ref)
        ones = jnp.ones((MOBA_BLOCK, HEAD_DIM), BF16)
        for jb in range(n_kb):
            rows = slice(jb * MOBA_BLOCK, (jb + 1) * MOBA_BLOCK)
            kb = k_ref[rows, :]
            kmean_ref[jb:jb + 1, :] = jnp.mean(kb.astype(F32), axis=0, keepdims=True)
            kt_ref[jb] = kb.T
            vb = v_ref[rows, :]
            for h in range(heads):
                vaug_ref[h, rows, :] = jnp.concatenate([vb[:, h * HEAD_DIM:(h + 1) * HEAD_DIM], ones], axis=1)

    blk = lax.broadcasted_iota(I32, (n_kb, tq), 0)
    row = lax.broadcasted_iota(I32, (tq, MOBA_BLOCK), 0)
    col = lax.broadcasted_iota(I32, (tq, MOBA_BLOCK), 1)
    causal = col <= row
    block_id = lax.broadcasted_iota(I32, (n_kb, MOBA_BLOCK), 0)

    qs, sel_bias, run_max = [], [], []
    for h in range(heads):
        hs = slice(h * HEAD_DIM, (h + 1) * HEAD_DIM)
        q = q_ref[:, hs] * (HEAD_DIM ** -0.5)
        qs.append(q)
        km_hi, km_lo = _split_bf16(kmean_ref[:, hs])
        gate = _dot_nt(km_hi, q) + _dot_nt(km_lo, q)
        beaten = jnp.zeros((n_kb, tq), I32)
        for jp in range(n_kb):
            g = gate[jp:jp + 1, :]
            beats = (g > gate) | ((g == gate) & (jp < blk))
            beaten = beaten + jnp.where(beats, 1, 0) * (jp < own).astype(I32)
        chosen = (beaten < MOBA_TOPK) & (blk < own)
        sel_bias.append(jnp.where(chosen, 0.0, MASKED).T.astype(BF16))
        run_max.append(jnp.full((tq, MOBA_BLOCK), NEG_INF, F32))

    def score_pass(t, running):
        out = list(running)
        for half in range(2):
            j = 2 * t + half
            valid = j < own
            jc = jnp.minimum(j, last)
            pick = (block_id == jc).astype(BF16)
            for h in range(heads):
                hs = slice(h * HEAD_DIM, (h + 1) * HEAD_DIM)
                s = _dot(qs[h], kt_ref[jc, hs, :]) + _dot(sel_bias[h], pick)
                s_ref[h, jnp.where(valid, j, n_kb)] = s
                out[h] = jnp.where(valid, jnp.maximum(out[h], s), out[h])
        return tuple(out)

    run_max = lax.fori_loop(0, (own + 1) // 2, score_pass, tuple(run_max))
    row_max = []
    for h in range(heads):
        hs = slice(h * HEAD_DIM, (h + 1) * HEAD_DIM)
        s = jnp.where(causal, _dot(qs[h], kt_ref[own, hs, :]), NEG_INF)
        s_ref[h, own] = s
        row_max.append(jnp.max(jnp.maximum(run_max[h], s), axis=1, keepdims=True))

    def softmax_pass(t, accs):
        out = list(accs)
        for half in range(2):
            j = 2 * t + half
            valid = j <= own
            jc = jnp.minimum(j, last)
            j0 = pl.multiple_of(jc * MOBA_BLOCK, MOBA_BLOCK)
            for h in range(heads):
                p = jnp.where(valid, jnp.exp(s_ref[h, jc] - row_max[h]), 0.0)
                out[h] = out[h] + _dot(p.astype(BF16), vaug_ref[h, pl.ds(j0, MOBA_BLOCK), :])
        return tuple(out)

    accs = lax.fori_loop(0, (own + 2) // 2, softmax_pass, tuple(jnp.zeros((tq, LANES), F32) for _ in range(heads)))
    o_ref[...] = jnp.concatenate([a[:, :HEAD_DIM] / a[:, HEAD_DIM:] for a in accs], axis=1).astype(o_ref.dtype)


def _moba_attention(proj3, q_blk, k_blk, v_blk):
    b, seq, _ = proj3.shape
    assert seq % MOBA_BLOCK == 0
    n_kb = seq // MOBA_BLOCK
    pairs = BRANCH_WIDTH // LANES
    kern = functools.partial(_moba_kernel, n_kb=n_kb)
    return pl.pallas_call(
        kern,
        out_shape=jax.ShapeDtypeStruct((b, seq, BRANCH_WIDTH), BF16),
        grid=(b, pairs, n_kb),
        in_specs=[pl.BlockSpec((None, MOBA_BLOCK, LANES), lambda bi, p, qi: (bi, qi, q_blk + p)),
                  pl.BlockSpec((None, seq, LANES), lambda bi, p, qi: (bi, 0, k_blk + p)),
                  pl.BlockSpec((None, seq, LANES), lambda bi, p, qi: (bi, 0, v_blk + p))],
        out_specs=pl.BlockSpec((None, MOBA_BLOCK, LANES), lambda bi, p, qi: (bi, qi, p)),
        scratch_shapes=[pltpu.VMEM((n_kb, LANES), F32),
                        pltpu.VMEM((n_kb, LANES, MOBA_BLOCK), BF16),
                        pltpu.VMEM((LANES // HEAD_DIM, seq, LANES), BF16),
                        pltpu.VMEM((LANES // HEAD_DIM, n_kb + 1, MOBA_BLOCK, MOBA_BLOCK), F32)],
        compiler_params=_cparams(3),
        name="moba_attention",
    )(proj3, proj3, proj3)


def _local_kernel(p_ref, ph_ref, ch_ref, chh_ref, cb_ref, cc_ref, cch_ref, pw_ref, ps_ref, cw_ref,
                  yp_ref, yc_ref, *, tm, tiles_per_seq):
    i = pl.program_id(0)
    seq_tile = i % tiles_per_seq
    has_left = seq_tile > 0

    def with_halo(halo_ref, body_ref):
        halo = jnp.where(has_left, halo_ref[...].astype(F32), 0.0)
        return jnp.concatenate([halo, body_ref[...].astype(F32)], axis=0)

    x = with_halo(ph_ref, p_ref)
    pos = seq_tile * tm + lax.broadcasted_iota(I32, (tm, 1), 0)
    for g, win in enumerate(POOL_WINDOWS):
        gs = slice(g * POOL_GROUP, (g + 1) * POOL_GROUP)
        xg = x[:, gs]
        s, k = xg, 1
        while k < win:
            s = s[k:] + s[:-k]
            k *= 2
        start = POOL_HALO + 1 - win
        window_sum = s[start:start + tm]
        count = jnp.minimum(pos + 1, win).astype(F32)
        mixed = window_sum / count - xg[POOL_HALO:]
        y = _dot(mixed.astype(BF16), pw_ref[g].astype(BF16)) * ps_ref[:, gs]
        yp_ref[:, gs] = y.astype(yp_ref.dtype)

    z = with_halo(chh_ref, ch_ref) * with_halo(cch_ref, cc_ref)
    y = jnp.zeros((tm, z.shape[1]), F32)
    for k in range(CONV_K):
        off = POOL_HALO - (CONV_K - 1) + k
        y = y + cw_ref[k:k + 1, :] * z[off:off + tm]
    yc_ref[...] = (cb_ref[...].astype(F32) * y).astype(yc_ref.dtype)


def _local_mixers(proj, pool_w, pool_scale, conv_w, seq, pool_blk, ch_blk, cb_blk, cc_blk, tm=512):
    n = proj.shape[0]
    w = BRANCH_WIDTH
    tm = min(tm, seq)
    tps = seq // tm
    hpt = tm // POOL_HALO

    def body(blk):
        return pl.BlockSpec((tm, w), lambda i: (i, blk))

    def halo(blk):
        return pl.BlockSpec((POOL_HALO, w), lambda i: (jnp.maximum(i * hpt - 1, 0), blk))

    kern = functools.partial(_local_kernel, tm=tm, tiles_per_seq=tps)
    return pl.pallas_call(
        kern,
        out_shape=(jax.ShapeDtypeStruct((n, w), BF16), jax.ShapeDtypeStruct((n, w), BF16)),
        grid=(n // tm,),
        in_specs=[body(pool_blk), halo(pool_blk), body(ch_blk), halo(ch_blk), body(cb_blk),
                  body(cc_blk), halo(cc_blk),
                  pl.BlockSpec(pool_w.shape, lambda i: (0, 0, 0)),
                  pl.BlockSpec((1, w), lambda i: (0, 0)),
                  pl.BlockSpec((CONV_K, w), lambda i: (0, 0))],
        out_specs=(pl.BlockSpec((tm, w), lambda i: (i, 0)), pl.BlockSpec((tm, w), lambda i: (i, 0))),
        compiler_params=_cparams(1),
        name="pool_conv",
    )(proj, proj, proj, proj, proj, proj, proj, pool_w, pool_scale.reshape(1, w), conv_w)


def _merge_kernel(b0, b1, b2, b3, g0, g1, g2, g3, wb_ref, gb_ref, o_ref, wbf_ref):
    @pl.when(pl.program_id(1) == 0)
    def _():
        wbf_ref[...] = wb_ref[...].astype(BF16)

    merged = None
    for nb, (br, gr) in enumerate(((b0, g0), (b1, g1), (b2, g2), (b3, g3))):
        lifted = _dot(br[...], wbf_ref[nb])
        gate = jax.nn.sigmoid(gr[...].astype(F32) + gb_ref[nb])
        term = gate * lifted
        merged = term if merged is None else merged + term
    o_ref[...] = merged.astype(o_ref.dtype)


def _merge(branches, proj, w_branch, gate_b, layer, gate_col, tm=1024, tn=512):
    n = proj.shape[0]
    d = w_branch.shape[3]
    assert gate_col % tn == 0 and d % tn == 0
    g_specs = [pl.BlockSpec((tm, tn), functools.partial(lambda j, i, nb: (i, (gate_col + nb * d) // tn + j), nb=nb))
               for nb in range(N_BRANCH)]
    b_specs = [pl.BlockSpec((tm, BRANCH_WIDTH), lambda j, i: (i, 0))] * N_BRANCH
    return pl.pallas_call(
        _merge_kernel,
        out_shape=jax.ShapeDtypeStruct((n, d), BF16),
        grid=(d // tn, n // tm),
        in_specs=b_specs + g_specs + [
            pl.BlockSpec((None, N_BRANCH, BRANCH_WIDTH, tn), lambda j, i: (layer, 0, 0, j)),
            pl.BlockSpec((N_BRANCH, 1, tn), lambda j, i: (0, 0, j))],
        out_specs=pl.BlockSpec((tm, tn), lambda j, i: (i, j)),
        scratch_shapes=[pltpu.VMEM((N_BRANCH, BRANCH_WIDTH, tn), BF16)],
        compiler_params=_cparams(2),
        name="gated_merge",
    )(*branches, proj, proj, proj, proj, w_branch, gate_b.reshape(N_BRANCH, 1, d))


def _post_norm(x, gate, y, g, b, alpha):
    h = alpha * x + (1.0 + gate) * y
    mu = jnp.mean(h, axis=1, keepdims=True)
    hc = h - mu
    var = jnp.mean(hc * hc, axis=1, keepdims=True)
    return hc * lax.rsqrt(var + LN_EPS) * g + b


def _outproj_kernel(mg_ref, w_ref, x_ref, gate_ref, g_ref, b_ref, sh_ref, sc_ref, rw_ref,
                    xo_ref, uo_ref, lg_ref, w_bf, *, alpha):
    @pl.when(pl.program_id(0) == 0)
    def _():
        w_bf[...] = w_ref[...].astype(BF16)

    y = _dot(mg_ref[...], w_bf[...])
    xn = _post_norm(x_ref[...], gate_ref[...], y, g_ref[...], b_ref[...], alpha)
    xo_ref[...] = xn
    u = xn * (1.0 + sc_ref[...]) + sh_ref[...]
    uo_ref[...] = u
    u_hi, u_lo = _split_bf16(u)
    w_hi, w_lo = _split_bf16(rw_ref[...])
    lg_ref[...] = _dot(u_hi, w_hi) + _dot(u_lo, w_hi) + _dot(u_hi, w_lo)


def _outproj_norm(merged, w_out, layer, x2, mod_l, ln_g, ln_b, router_w, alpha, seq, tm=256):
    n, d = x2.shape
    e = router_w.shape[2]
    tm = min(tm, seq)
    tps = seq // tm
    kern = functools.partial(_outproj_kernel, alpha=alpha)
    vec = lambda: pl.BlockSpec((None, 1, d), lambda i: (layer, 0, 0))
    row = lambda: pl.BlockSpec((tm, d), lambda i: (i, 0))
    once = pl.Buffered(1)
    return pl.pallas_call(
        kern,
        out_shape=(jax.ShapeDtypeStruct((n, d), F32), jax.ShapeDtypeStruct((n, d), F32),
                   jax.ShapeDtypeStruct((n, e), F32)),
        grid=(n // tm,),
        in_specs=[row(),
                  pl.BlockSpec((None, d, d), lambda i: (layer, 0, 0), pipeline_mode=once),
                  row(),
                  _mod_spec(2, d, tps), vec(), vec(), _mod_spec(3, d, tps), _mod_spec(4, d, tps),
                  pl.BlockSpec((None, d, e), lambda i: (layer, 0, 0), pipeline_mode=once)],
        out_specs=(row(), row(), pl.BlockSpec((tm, e), lambda i: (i, 0))),
        scratch_shapes=[pltpu.VMEM((d, d), BF16)],
        compiler_params=_cparams(1),
        name="out_proj_norm",
    )(merged, w_out, x2, mod_l, ln_g.reshape(-1, 1, d), ln_b.reshape(-1, 1, d), mod_l, mod_l, router_w)


def _route_kernel(lg_ref, bias_ref, idx_ref, w_ref, pos_ref, cnt_ref, carry_ref, *, tm):
    i = pl.program_id(0)

    @pl.when(i == 0)
    def _():
        carry_ref[...] = jnp.zeros_like(carry_ref)

    aff = jax.nn.sigmoid(lg_ref[...].T)
    e = aff.shape[0]
    sub = lax.broadcasted_iota(I32, (e, tm), 0).astype(F32)
    work = aff + bias_ref[...]
    picks = []
    chosen_f = jnp.zeros((e, tm), F32)
    for _ in range(TOP_K):
        best = jnp.max(work, axis=0, keepdims=True)
        pick = jnp.min(jnp.where(work == best, sub, float(e)), axis=0, keepdims=True)
        onehot = sub == pick
        picks.append((pick, onehot))
        chosen_f = jnp.where(onehot, 1.0, chosen_f)
        work = jnp.where(onehot, NEG_INF, work)

    sel_aff = chosen_f * aff
    dense_w = sel_aff / jnp.sum(sel_aff, axis=0, keepdims=True) * ROUTED_SCALE
    r = lax.broadcasted_iota(I32, (tm, tm), 0)
    c = lax.broadcasted_iota(I32, (tm, tm), 1)
    earlier = (r < c).astype(BF16)
    rank = _dot(chosen_f.astype(BF16), earlier) + carry_ref[...]
    carry_ref[...] += jnp.sum(chosen_f, axis=1, keepdims=True)
    cnt_ref[...] = carry_ref[...].astype(I32)

    slot = lax.broadcasted_iota(I32, (TOP_K, tm), 0)
    idx8 = jnp.zeros((TOP_K, tm), F32)
    w8 = jnp.zeros((TOP_K, tm), F32)
    pos8 = jnp.zeros((TOP_K, tm), F32)
    for kk, (pick, onehot) in enumerate(picks):
        idx8 = jnp.where(slot == kk, pick, idx8)
        w8 = jnp.where(slot == kk, jnp.sum(jnp.where(onehot, dense_w, 0.0), axis=0, keepdims=True), w8)
        pos8 = jnp.where(slot == kk, jnp.sum(jnp.where(onehot, rank, 0.0), axis=0, keepdims=True), pos8)
    idx_ref[...] = idx8.astype(I32)
    w_ref[...] = w8
    pos_ref[...] = pos8.astype(I32)


def _route(logits, router_bias, layer, tm=512):
    n, e = logits.shape
    kern = functools.partial(_route_kernel, tm=tm)
    slots = lambda: pl.BlockSpec((TOP_K, tm), lambda i: (0, i))
    return pl.pallas_call(
        kern,
        out_shape=(jax.ShapeDtypeStruct((TOP_K, n), I32), jax.ShapeDtypeStruct((TOP_K, n), F32),
                   jax.ShapeDtypeStruct((TOP_K, n), I32), jax.ShapeDtypeStruct((e, 1), I32)),
        grid=(n // tm,),
        in_specs=[pl.BlockSpec((tm, e), lambda i: (i, 0)),
                  pl.BlockSpec((None, e, 1), lambda i: (layer, 0, 0))],
        out_specs=(slots(), slots(), slots(), pl.BlockSpec((e, 1), lambda i: (0, 0))),
        scratch_shapes=[pltpu.VMEM((e, 1), F32)],
        compiler_params=_cparams(1),
        name="route",
    )(logits, router_bias.reshape(-1, e, 1))


def _dispatch_kernel(cnt_ref, start_ref, nu_ref, u_ref, idx_ref, pos_ref, dest_ref, xs_hbm,
                     dest_smem, zrow_ref, zblk_ref, idx_sem, row_sem, zero_sem,
                     *, tm, tr, n_tiles, n_blocks, n_experts):
    i = pl.program_id(0)
    idx = idx_ref[...]
    dest = pos_ref[...]
    for e in range(n_experts):
        dest = dest + jnp.where(idx == e, start_ref[e], 0)
    dest_ref[...] = dest
    idx_cp = pltpu.make_async_copy(dest_ref, dest_smem, idx_sem)
    idx_cp.start()
    idx_cp.wait()

    def issue(g, carry):
        base = pl.multiple_of(g * 8, 8)
        for u in range(8):
            for kk in range(TOP_K):
                dst = dest_smem[kk, base + u]
                pltpu.make_async_copy(u_ref.at[pl.ds(base + u, 1)], xs_hbm.at[pl.ds(dst, 1)],
                                      row_sem).start(priority=kk % 2)
        return carry

    lax.fori_loop(0, tm // 8, issue, 0)
    for kk in range(TOP_K):
        pltpu.make_async_copy(u_ref, xs_hbm.at[pl.ds(0, tm)], row_sem).wait()

    @pl.when(i == n_tiles - 1)
    def _():
        zrow_ref[...] = jnp.zeros_like(zrow_ref)
        zblk_ref[...] = jnp.zeros_like(zblk_ref)

        def pad_expert(e, carry):
            cnt = cnt_ref[e]
            first = start_ref[e] + cnt
            n_pad = (tr - cnt % tr) % tr

            def start(r, c):
                pltpu.make_async_copy(zrow_ref.at[pl.ds(0, 1)], xs_hbm.at[pl.ds(first + r, 1)], zero_sem).start()
                return c

            def wait(r, c):
                pltpu.make_async_copy(zrow_ref.at[pl.ds(0, 1)], xs_hbm.at[pl.ds(0, 1)], zero_sem).wait()
                return c

            lax.fori_loop(0, n_pad, start, 0)
            lax.fori_loop(0, n_pad, wait, 0)
            return carry

        lax.fori_loop(0, n_experts, pad_expert, 0)

        def tail_start(blk, c):
            row0 = pl.multiple_of(blk * tr, tr)
            pltpu.make_async_copy(zblk_ref, xs_hbm.at[pl.ds(row0, tr)], zero_sem).start()
            return c

        def tail_wait(blk, c):
            pltpu.make_async_copy(zblk_ref, xs_hbm.at[pl.ds(0, tr)], zero_sem).wait()
            return c

        lax.fori_loop(nu_ref[0], n_blocks, tail_start, 0)
        lax.fori_loop(nu_ref[0], n_blocks, tail_wait, 0)


def _dispatch(u2, idx_t, pos_t, counts, start_padded, n_used, tr, n_blocks, tm=512):
    n, d = u2.shape
    n_tiles = n // tm
    kern = functools.partial(_dispatch_kernel, tm=tm, tr=tr, n_tiles=n_tiles, n_blocks=n_blocks,
                             n_experts=counts.shape[0])
    slots = lambda: pl.BlockSpec((TOP_K, tm), lambda i, *_: (0, i))
    grid_spec = pltpu.PrefetchScalarGridSpec(
        num_scalar_prefetch=3,
        grid=(n_tiles,),
        in_specs=[pl.BlockSpec((tm, d), lambda i, *_: (i, 0)), slots(), slots()],
        out_specs=(slots(), pl.BlockSpec(memory_space=pl.ANY)),
        scratch_shapes=[pltpu.SMEM((TOP_K, tm), I32), pltpu.VMEM((8, d), F32), pltpu.VMEM((tr, d), F32),
                        pltpu.SemaphoreType.DMA(()), pltpu.SemaphoreType.DMA(()), pltpu.SemaphoreType.DMA(())])
    dest_t, xs = pl.pallas_call(
        kern,
        out_shape=(jax.ShapeDtypeStruct((TOP_K, n), I32), jax.ShapeDtypeStruct((n_blocks * tr, d), F32)),
        grid_spec=grid_spec,
        compiler_params=_cparams(1),
        name="dispatch",
    )(counts, start_padded, n_used, u2, idx_t, pos_t)
    return xs, dest_t


def _experts_kernel(be_ref, bs_ref, nxt_ref, nu_ref, x_ref, wg_hbm, wu_hbm, wd_hbm, y_ref,
                    wg_buf, wu_buf, wd_buf, wgu_bf, wd_bf, slot_ref, sems, *, hid, layer):
    i = pl.program_id(0)

    def fetch(e, s):
        return (pltpu.make_async_copy(wg_hbm.at[layer, e], wg_buf.at[s], sems.at[s, 0]),
                pltpu.make_async_copy(wu_hbm.at[layer, e], wu_buf.at[s], sems.at[s, 1]),
                pltpu.make_async_copy(wd_hbm.at[layer, e], wd_buf.at[s], sems.at[s, 2]))

    @pl.when(i == 0)
    def _():
        slot_ref[0] = 0
        for cp in fetch(be_ref[0], 0):
            cp.start()

    @pl.when(i < nu_ref[0])
    def _():
        e = be_ref[i]
        changed = jnp.logical_or(i == 0, e != be_ref[jnp.maximum(i - 1, 0)])

        @pl.when(changed)
        def _():
            s = slot_ref[0]
            for cp in fetch(e, s):
                cp.wait()
            nxt = nxt_ref[e]

            @pl.when(nxt >= 0)
            def _():
                for cp in fetch(nxt, 1 - s):
                    cp.start()

            wgu_bf[:, :hid] = wg_buf[s].astype(BF16)
            wgu_bf[:, hid:] = wu_buf[s].astype(BF16)
            wd_bf[...] = wd_buf[s].astype(BF16)
            slot_ref[0] = 1 - s

        gu = _dot(x_ref[...].astype(BF16), wgu_bf[...])
        g = gu[:, :hid]
        h = (g * jax.nn.sigmoid(g)) * gu[:, hid:]
        y_ref[...] = _dot(h.astype(BF16), wd_bf[...])

    @pl.when(i >= nu_ref[0])
    def _():
        y_ref[...] = jnp.zeros_like(y_ref)


def _experts(xs, block_e, block_src, next_e, n_used, w_gate, w_up, w_down, layer, tr):
    rows, d = xs.shape
    n_blocks = rows // tr
    hid = w_gate.shape[3]
    kern = functools.partial(_experts_kernel, hid=hid, layer=layer)
    hbm = lambda: pl.BlockSpec(memory_space=pl.ANY)
    grid_spec = pltpu.PrefetchScalarGridSpec(
        num_scalar_prefetch=4,
        grid=(n_blocks,),
        in_specs=[pl.BlockSpec((tr, d), lambda i, be, bs, nx, nu: (bs[i], 0)), hbm(), hbm(), hbm()],
        out_specs=pl.BlockSpec((tr, d), lambda i, be, bs, nx, nu: (i, 0)),
        scratch_shapes=[pltpu.VMEM((2, d, hid), F32), pltpu.VMEM((2, d, hid), F32), pltpu.VMEM((2, hid, d), F32),
                        pltpu.VMEM((d, 2 * hid), BF16), pltpu.VMEM((hid, d), BF16),
                        pltpu.SMEM((1,), I32), pltpu.SemaphoreType.DMA((2, 3))])
    return pl.pallas_call(
        kern,
        out_shape=jax.ShapeDtypeStruct((rows, d), F32),
        grid_spec=grid_spec,
        compiler_params=_cparams(1),
        name="routed_experts",
    )(block_e, block_src, next_e, n_used, xs, w_gate, w_up, w_down)


def _shared_kernel(u_ref, wg_ref, wu_ref, wd_ref, o_ref, wg_bf, wu_bf, wd_bf):
    @pl.when(pl.program_id(0) == 0)
    def _():
        wg_bf[...] = wg_ref[...].astype(BF16)
        wu_bf[...] = wu_ref[...].astype(BF16)
        wd_bf[...] = wd_ref[...].astype(BF16)

    xb = u_ref[...].astype(BF16)
    g = _dot(xb, wg_bf[...])
    h = (g * jax.nn.sigmoid(g)) * _dot(xb, wu_bf[...])
    o_ref[...] = _dot(h.astype(BF16), wd_bf[...])


def _shared_expert(u2, sh_gate, sh_up, sh_down, layer, tm=512):
    n, d = u2.shape
    hid = sh_gate.shape[2]
    return pl.pallas_call(
        _shared_kernel,
        out_shape=jax.ShapeDtypeStruct((n, d), F32),
        grid=(n // tm,),
        in_specs=[pl.BlockSpec((tm, d), lambda i: (i, 0)),
                  pl.BlockSpec((None, d, hid), lambda i: (layer, 0, 0)),
                  pl.BlockSpec((None, d, hid), lambda i: (layer, 0, 0)),
                  pl.BlockSpec((None, hid, d), lambda i: (layer, 0, 0))],
        out_specs=pl.BlockSpec((tm, d), lambda i: (i, 0)),
        scratch_shapes=[pltpu.VMEM((d, hid), BF16), pltpu.VMEM((d, hid), BF16), pltpu.VMEM((hid, d), BF16)],
        compiler_params=_cparams(1),
        name="shared_expert",
    )(u2, sh_gate, sh_up, sh_down)


def _combine_kernel(dest_ref, y_hbm, w8_ref, shd_ref, x_ref, gate_ref, g_ref, b_ref, sh_ref, sc_ref,
                    xo_ref, uo_ref, idx_smem, ybuf, idx_sem, row_sem, *, tm, n_tiles, alpha):
    i = pl.program_id(0)
    slot = i % 2

    def gather(tile, s):
        cp = pltpu.make_async_copy(dest_ref.at[tile], idx_smem.at[s], idx_sem)
        cp.start()
        cp.wait()

        def issue(g, carry):
            base = pl.multiple_of(g * 8, 8)
            for u in range(8):
                for kk in range(TOP_K):
                    src = idx_smem[s, kk, base + u]
                    pltpu.make_async_copy(y_hbm.at[pl.ds(src, 1)], ybuf.at[s, kk, pl.ds(base + u, 1)],
                                          row_sem.at[s]).start(priority=kk % 2)
            return carry

        lax.fori_loop(0, tm // 8, issue, 0)

    @pl.when(i == 0)
    def _():
        gather(0, 0)

    @pl.when(i + 1 < n_tiles)
    def _():
        gather(i + 1, 1 - slot)

    for kk in range(TOP_K):
        pltpu.make_async_copy(y_hbm.at[pl.ds(0, tm)], ybuf.at[slot, kk], row_sem.at[slot]).wait()
    w8 = w8_ref[...]
    ffn = shd_ref[...]
    for kk in range(TOP_K):
        ffn = ffn + w8[:, kk:kk + 1] * ybuf[slot, kk]
    xn = _post_norm(x_ref[...], gate_ref[...], ffn, g_ref[...], b_ref[...], alpha)
    xo_ref[...] = xn
    uo_ref[...] = (xn * (1.0 + sc_ref[...]) + sh_ref[...]).astype(uo_ref.dtype)


def _combine_norm(dest_t, y_sorted, w8, shared, x2, mod_l, mod_next, ln_g, ln_b, layer, alpha, seq, tm=128):
    n, d = x2.shape
    n_tiles = n // tm
    tm = min(tm, seq)
    tps = seq // tm
    kern = functools.partial(_combine_kernel, tm=tm, n_tiles=n_tiles, alpha=alpha)
    row = lambda: pl.BlockSpec((tm, d), lambda i: (i, 0))
    vec = lambda: pl.BlockSpec((None, 1, d), lambda i: (layer, 0, 0))
    return pl.pallas_call(
        kern,
        out_shape=(jax.ShapeDtypeStruct((n, d), F32), jax.ShapeDtypeStruct((n, d), BF16)),
        grid=(n_tiles,),
        in_specs=[pl.BlockSpec((n_tiles, TOP_K, tm), lambda i: (0, 0, 0)),
                  pl.BlockSpec(memory_space=pl.ANY),
                  pl.BlockSpec((tm, TOP_K), lambda i: (i, 0)),
                  row(), row(), _mod_spec(5, d, tps), vec(), vec(),
                  _mod_spec(0, d, tps), _mod_spec(1, d, tps)],
        out_specs=(row(), row()),
        scratch_shapes=[pltpu.SMEM((2, TOP_K, tm), I32), pltpu.VMEM((2, TOP_K, tm, d), F32),
                        pltpu.SemaphoreType.DMA(()), pltpu.SemaphoreType.DMA((2,))],
        compiler_params=_cparams(1),
        name="combine_norm",
    )(dest_t.reshape(TOP_K, n_tiles, tm).transpose(1, 0, 2), y_sorted, w8, shared, x2, mod_l,
      ln_g.reshape(-1, 1, d), ln_b.reshape(-1, 1, d), mod_next, mod_next)


def _dispatch_plan(counts, tr, n_blocks):
    padded = (counts + tr - 1) // tr * tr
    end_padded = jnp.cumsum(padded)
    start_padded = (end_padded - padded).astype(I32)
    n_used = (end_padded[-1] // tr).astype(I32)
    block = jnp.arange(n_blocks, dtype=I32)
    block_src = jnp.minimum(block, n_used - 1)
    block_e = jnp.sum((block_src[:, None] * tr >= end_padded[None, :]).astype(I32), axis=1)
    n_e = counts.shape[0]
    ids = jnp.where(counts > 0, jnp.arange(n_e, dtype=I32), n_e)
    later = jnp.concatenate([lax.cummin(ids[::-1])[::-1][1:], jnp.full((1,), n_e, I32)])
    next_e = jnp.where(later < n_e, later, -1).astype(I32)
    return start_padded, block_e.astype(I32), block_src, next_e, n_used.reshape(1)


def kernel(x, c, ada_w, ada_b, w_in, gate_b, pool_w, pool_scale, conv_w, w_branch, w_out, ln1_g, ln1_b,
           router_w, router_bias, exp_w_gate, exp_w_up, exp_w_down, sh_w_gate, sh_w_up, sh_w_down, ln2_g, ln2_b):
    b, seq, d = x.shape
    depth = ada_w.shape[0]
    n = b * seq
    alpha = (2 * depth) ** 0.25
    expert_rows = 256
    n_blocks = -(-(n * TOP_K + N_EXPERTS * (expert_rows - 1)) // expert_rows)

    w = BRANCH_WIDTH
    cb = w // LANES
    sq, sk, sv = 0, cb, 2 * cb
    pool_col = 3 * w
    mq_col = 4 * w
    mq, mk, mv = mq_col // LANES, mq_col // LANES + cb, mq_col // LANES + 2 * cb
    conv_col = 7 * w
    gate_col = 10 * w

    mod = _ada(c, ada_w, ada_b)
    cos_t, sin_t = _rope_tables(seq)
    x2 = x.reshape(n, d)
    u = _modulate(x2, mod[0], seq)
    for l in range(depth):
        proj = _inproj(u, w_in, l, cos_t, sin_t, seq, rot_col=mq_col)
        proj3 = proj.reshape(b, seq, -1)
        y_sb = _sb_attention(proj3, sq, sk, sv).reshape(n, w)
        y_mb = _moba_attention(proj3, mq, mk, mv).reshape(n, w)
        y_pool, y_cv = _local_mixers(proj, pool_w[l], pool_scale[l], conv_w[l], seq,
                                     pool_col // w, conv_col // w, conv_col // w + 1, conv_col // w + 2)
        merged = _merge((y_sb, y_pool, y_mb, y_cv), proj, w_branch, gate_b[l], l, gate_col)
        x2, u2, logits = _outproj_norm(merged, w_out, l, x2, mod[l], ln1_g, ln1_b, router_w, alpha, seq)
        idx_t, w_t, pos_t, counts = _route(logits, router_bias, l)
        counts = counts[:, 0]
        start_padded, block_e, block_src, next_e, n_used = _dispatch_plan(counts, expert_rows, n_blocks)
        xs, dest_t = _dispatch(u2, idx_t, pos_t, counts, start_padded, n_used, expert_rows, n_blocks)
        y_sorted = _experts(xs, block_e, block_src, next_e, n_used, exp_w_gate, exp_w_up, exp_w_down, l,
                            expert_rows)
        shared = _shared_expert(u2, sh_w_gate, sh_w_up, sh_w_down, l)
        x2, u = _combine_norm(dest_t, y_sorted, w_t.T, shared, x2, mod[l], mod[(l + 1) % depth],
                              ln2_g, ln2_b, l, alpha, seq)
    return x2.reshape(b, seq, d)
```

```python
import functools

import numpy as np
import jax
import jax.numpy as jnp
from jax import lax
from jax.experimental import pallas as pl
from jax.experimental.pallas import tpu as pltpu

F32 = jnp.float32
BF16 = jnp.bfloat16
I32 = jnp.int32
U32 = jnp.uint32

HEAD_DIM = 64
N_BRANCH = 4
BRANCH_WIDTH = 512
POOL_WINDOWS = (2, 4, 8, 16)
POOL_GROUP = 128
POOL_HALO = 16
CONV_K = 3
MOBA_BLOCK = 256
MOBA_TOPK = 3
ROPE_THETA = 10000.0
N_EXPERTS = 64
TOP_K = 8
EXPERT_HIDDEN = 384
ROUTED_SCALE = 2.5
LN_EPS = 1e-5
NEG_INF = float("-inf")
SB_DEAD_LOG = -100.0
MASKED = -1e30

LANES = 128
VMEM_LIMIT = 56 * 1024 * 1024


def _cparams(n_axes, vmem=VMEM_LIMIT):
    return pltpu.CompilerParams(dimension_semantics=("arbitrary",) * n_axes, vmem_limit_bytes=vmem)


def _dot(a, b):
    return jnp.dot(a, b, preferred_element_type=F32)


def _dot_nt(a, b):
    return lax.dot_general(a, b, (((1,), (1,)), ((), ())), preferred_element_type=F32)


def _split_bf16(x):
    hi = x.astype(BF16)
    lo = (x - hi.astype(F32)).astype(BF16)
    return hi, lo


def _ada_kernel(c_ref, w_ref, b_ref, o_ref):
    c = c_ref[...]
    cond = c * jax.nn.sigmoid(c)
    o_ref[...] = jnp.dot(cond, w_ref[...], preferred_element_type=F32,
                         precision=lax.Precision.HIGHEST) + b_ref[...]


def _ada(c, ada_w, ada_b):
    depth, d, d6 = ada_w.shape
    b = c.shape[0]
    rows = 8
    tn = 1024
    c8 = jnp.zeros((rows, d), F32).at[:b].set(c)
    out = pl.pallas_call(
        _ada_kernel,
        out_shape=jax.ShapeDtypeStruct((depth, rows, d6), F32),
        grid=(depth, d6 // tn),
        in_specs=[pl.BlockSpec((rows, d), lambda l, n: (0, 0)),
                  pl.BlockSpec((None, d, tn), lambda l, n: (l, 0, n)),
                  pl.BlockSpec((None, 1, tn), lambda l, n: (l, 0, n))],
        out_specs=pl.BlockSpec((None, rows, tn), lambda l, n: (l, 0, n)),
        compiler_params=_cparams(2),
        name="ada_mod",
    )(c8, ada_w, ada_b.reshape(depth, 1, d6))
    return out[:, :b].reshape(depth, b, 6, 1, d)


def _mod_spec(which, d, tiles_per_seq):
    return pl.BlockSpec((None, None, 1, d), lambda i, *_: (i // tiles_per_seq, which, 0, 0))


def _modulate_kernel(x_ref, sh_ref, sc_ref, u_ref):
    u_ref[...] = (x_ref[...] * (1.0 + sc_ref[...]) + sh_ref[...]).astype(u_ref.dtype)


def _modulate(x2, mod_l, seq, tm=512):
    n, d = x2.shape
    tm = min(tm, seq)
    tps = seq // tm
    return pl.pallas_call(
        _modulate_kernel,
        out_shape=jax.ShapeDtypeStruct((n, d), BF16),
        grid=(n // tm,),
        in_specs=[pl.BlockSpec((tm, d), lambda i: (i, 0)), _mod_spec(0, d, tps), _mod_spec(1, d, tps)],
        out_specs=pl.BlockSpec((tm, d), lambda i: (i, 0)),
        compiler_params=_cparams(1),
        name="modulate",
    )(x2, mod_l, mod_l)


def _inproj_kernel(u_ref, w_ref, cos_ref, sin_ref, o_ref, wbf_ref, *, rot_tile, tn):
    n = pl.program_id(0)
    m = pl.program_id(1)

    @pl.when(m == 0)
    def _():
        wbf_ref[...] = w_ref[...].astype(BF16)

    acc = _dot(u_ref[...], wbf_ref[...])

    @pl.when(n == rot_tile)
    def _():
        reps = tn // LANES
        cos = jnp.concatenate([cos_ref[...]] * reps, axis=1)
        sin = jnp.concatenate([sin_ref[...]] * reps, axis=1)
        lane = lax.broadcasted_iota(I32, acc.shape, 1)
        first_half = (lane % HEAD_DIM) < (HEAD_DIM // 2)
        half = HEAD_DIM // 2
        partner = jnp.where(first_half, pltpu.roll(acc, tn - half, 1), pltpu.roll(acc, half, 1))
        o_ref[...] = (acc * cos + partner * sin).astype(o_ref.dtype)

    @pl.when(n != rot_tile)
    def _():
        o_ref[...] = acc.astype(o_ref.dtype)


def _inproj(u, w_in, layer, cos_t, sin_t, seq, rot_col, tm=1024, tn=1024):
    n, d = u.shape
    cols = w_in.shape[2]
    tm = min(tm, seq)
    tps = seq // tm
    assert rot_col % tn == 0
    kern = functools.partial(_inproj_kernel, rot_tile=rot_col // tn, tn=tn)
    return pl.pallas_call(
        kern,
        out_shape=jax.ShapeDtypeStruct((n, cols), BF16),
        grid=(cols // tn, n // tm),
        in_specs=[pl.BlockSpec((tm, d), lambda j, i: (i, 0)),
                  pl.BlockSpec((None, d, tn), lambda j, i: (layer, 0, j)),
                  pl.BlockSpec((tm, LANES), lambda j, i: (i % tps, 0)),
                  pl.BlockSpec((tm, LANES), lambda j, i: (i % tps, 0))],
        out_specs=pl.BlockSpec((tm, tn), lambda j, i: (i, j)),
        scratch_shapes=[pltpu.VMEM((d, tn), BF16)],
        compiler_params=_cparams(2),
        name="in_proj",
    )(u, w_in, cos_t, sin_t)


def _rope_tables(seq):
    half = HEAD_DIM // 2
    inv_freq = ROPE_THETA ** (-jnp.arange(half, dtype=F32) / half)
    ang = jnp.arange(seq, dtype=F32)[:, None] * inv_freq[None, :]
    cos, sin = jnp.cos(ang), jnp.sin(ang)
    cos_h = jnp.concatenate([cos, cos], axis=1)
    sin_h = jnp.concatenate([-sin, sin], axis=1)
    reps = LANES // HEAD_DIM
    return jnp.tile(cos_h, (1, reps)), jnp.tile(sin_h, (1, reps))


def _sb_kernel(q_ref, k_ref, v_ref, o_ref, *, tq, tk):
    qi = pl.program_id(2)
    q0 = qi * tq
    r = lax.broadcasted_iota(I32, (tk, tk), 0)
    c = lax.broadcasted_iota(I32, (tk, tk), 1)
    later = (r > c).astype(BF16)
    heads = LANES // HEAD_DIM
    qs = [q_ref[:, h * HEAD_DIM:(h + 1) * HEAD_DIM] * (HEAD_DIM ** -0.5) for h in range(heads)]

    def block(j0, h, carry, acc, masked):
        k = k_ref[pl.ds(j0, tk), h * HEAD_DIM:(h + 1) * HEAD_DIM]
        v = v_ref[pl.ds(j0, tk), h * HEAD_DIM:(h + 1) * HEAD_DIM]
        z = _dot_nt(qs[h], k)
        softplus = jnp.maximum(z, 0.0) + jnp.log(1.0 + jnp.exp(-jnp.abs(z)))
        log_keep = -softplus
        if masked:
            qpos = q0 + lax.broadcasted_iota(I32, (tq, tk), 0)
            kpos = j0 + lax.broadcasted_iota(I32, (tq, tk), 1)
            past = kpos < qpos
            log_keep = jnp.where(past, log_keep, 0.0)
        hi, lo = _split_bf16(log_keep)
        between = _dot(hi, later) + _dot(lo, later) + carry
        w = jnp.exp((z - softplus) + between)
        if masked:
            w = jnp.where(past, w, 0.0)
        acc = acc + _dot(w.astype(BF16), v)
        carry = carry + jnp.sum(log_keep, axis=1, keepdims=True)
        return carry, acc

    state = [(jnp.zeros((tq, 1), F32), jnp.zeros((tq, HEAD_DIM), F32)) for _ in range(heads)]
    n_diag = tq // tk
    for d in range(n_diag - 1, -1, -1):
        j0 = pl.multiple_of(q0 + d * tk, tk)
        state = [block(j0, h, *state[h], True) for h in range(heads)]

    def live(pairs):
        return functools.reduce(jnp.maximum, [jnp.max(c) for c, _ in pairs])

    def cond(st):
        return jnp.logical_and(st[0] < qi, st[1] > SB_DEAD_LOG)

    def body(st):
        jj = st[0]
        new = [(st[2 + 2 * h], st[3 + 2 * h]) for h in range(heads)]
        for part in range(n_diag):
            j0 = pl.multiple_of(q0 - (jj * n_diag + part + 1) * tk, tk)
            new = [block(j0, h, *new[h], False) for h in range(heads)]
        return (jj + 1, live(new)) + tuple(x for pair in new for x in pair)

    final = lax.while_loop(cond, body, (jnp.int32(0), live(state)) + tuple(x for pair in state for x in pair))
    o_ref[...] = jnp.concatenate([final[3 + 2 * h] for h in range(heads)], axis=1).astype(o_ref.dtype)


def _sb_attention(proj3, q_blk, k_blk, v_blk, tq=256, tk=128):
    b, seq, _ = proj3.shape
    pairs = BRANCH_WIDTH // LANES
    kern = functools.partial(_sb_kernel, tq=tq, tk=tk)
    return pl.pallas_call(
        kern,
        out_shape=jax.ShapeDtypeStruct((b, seq, BRANCH_WIDTH), BF16),
        grid=(b, pairs, seq // tq),
        in_specs=[pl.BlockSpec((None, tq, LANES), lambda bi, p, qi: (bi, qi, q_blk + p)),
                  pl.BlockSpec((None, seq, LANES), lambda bi, p, qi: (bi, 0, k_blk + p)),
                  pl.BlockSpec((None, seq, LANES), lambda bi, p, qi: (bi, 0, v_blk + p))],
        out_specs=pl.BlockSpec((None, tq, LANES), lambda bi, p, qi: (bi, qi, p)),
        compiler_params=_cparams(3),
        name="sb_attention",
    )(proj3, proj3, proj3)


def _moba_kernel(q_ref, k_ref, v_ref, o_ref, kmean_ref, kt_ref, vaug_ref, s_ref, *, n_kb):
    own = pl.program_id(2)
    tq = MOBA_BLOCK
    heads = LANES // HEAD_DIM
    last = n_kb - 1

    @pl.when(own == 0)
    def _():
        s_ref[...] = jnp.zeros_like(s_ref)
        ones = jnp.ones((MOBA_BLOCK, HEAD_DIM), BF16)
        for jb in range(n_kb):
            rows = slice(jb * MOBA_BLOCK, (jb + 1) * MOBA_BLOCK)
            kb = k_ref[rows, :]
            kmean_ref[jb:jb + 1, :] = jnp.mean(kb.astype(F32), axis=0, keepdims=True)
            kt_ref[jb] = kb.T
            vb = v_ref[rows, :]
            for h in range(heads):
                vaug_ref[h, rows, :] = jnp.concatenate([vb[:, h * HEAD_DIM:(h + 1) * HEAD_DIM], ones], axis=1)

    blk = lax.broadcasted_iota(I32, (n_kb, tq), 0)
    row = lax.broadcasted_iota(I32, (tq, MOBA_BLOCK), 0)
    col = lax.broadcasted_iota(I32, (tq, MOBA_BLOCK), 1)
    causal = col <= row
    block_id = lax.broadcasted_iota(I32, (n_kb, MOBA_BLOCK), 0)

    qs, sel_bias, run_max = [], [], []
    for h in range(heads):
        hs = slice(h * HEAD_DIM, (h + 1) * HEAD_DIM)
        q = q_ref[:, hs] * (HEAD_DIM ** -0.5)
        qs.append(q)
        km_hi, km_lo = _split_bf16(kmean_ref[:, hs])
        gate = _dot_nt(km_hi, q) + _dot_nt(km_lo, q)
        beaten = jnp.zeros((n_kb, tq), I32)
        for jp in range(n_kb):
            g = gate[jp:jp + 1, :]
            beats = (g > gate) | ((g == gate) & (jp < blk))
            beaten = beaten + jnp.where(beats, 1, 0) * (jp < own).astype(I32)
        chosen = (beaten < MOBA_TOPK) & (blk < own)
        sel_bias.append(jnp.where(chosen, 0.0, MASKED).T.astype(BF16))
        run_max.append(jnp.full((tq, MOBA_BLOCK), NEG_INF, F32))

    def score_pass(t, running):
        out = list(running)
        for half in range(2):
            j = 2 * t + half
            valid = j < own
            jc = jnp.minimum(j, last)
            pick = (block_id == jc).astype(BF16)
            for h in range(heads):
                hs = slice(h * HEAD_DIM, (h + 1) * HEAD_DIM)
                s = _dot(qs[h], kt_ref[jc, hs, :]) + _dot(sel_bias[h], pick)
                s_ref[h, jnp.where(valid, j, n_kb)] = s
                out[h] = jnp.where(valid, jnp.maximum(out[h], s), out[h])
        return tuple(out)

    run_max = lax.fori_loop(0, (own + 1) // 2, score_pass, tuple(run_max))
    row_max = []
    for h in range(heads):
        hs = slice(h * HEAD_DIM, (h + 1) * HEAD_DIM)
        s = jnp.where(causal, _dot(qs[h], kt_ref[own, hs, :]), NEG_INF)
        s_ref[h, own] = s
        row_max.append(jnp.max(jnp.maximum(run_max[h], s), axis=1, keepdims=True))

    def softmax_pass(t, accs):
        out = list(accs)
        for half in range(2):
            j = 2 * t + half
            valid = j <= own
            jc = jnp.minimum(j, last)
            j0 = pl.multiple_of(jc * MOBA_BLOCK, MOBA_BLOCK)
            for h in range(heads):
                p = jnp.where(valid, jnp.exp(s_ref[h, jc] - row_max[h]), 0.0)
                out[h] = out[h] + _dot(p.astype(BF16), vaug_ref[h, pl.ds(j0, MOBA_BLOCK), :])
        return tuple(out)

    accs = lax.fori_loop(0, (own + 2) // 2, softmax_pass, tuple(jnp.zeros((tq, LANES), F32) for _ in range(heads)))
    o_ref[...] = jnp.concatenate([a[:, :HEAD_DIM] / a[:, HEAD_DIM:] for a in accs], axis=1).astype(o_ref.dtype)


def _moba_attention(proj3, q_blk, k_blk, v_blk):
    b, seq, _ = proj3.shape
    assert seq % MOBA_BLOCK == 0
    n_kb = seq // MOBA_BLOCK
    pairs = BRANCH_WIDTH // LANES
    kern = functools.partial(_moba_kernel, n_kb=n_kb)
    return pl.pallas_call(
        kern,
        out_shape=jax.ShapeDtypeStruct((b, seq, BRANCH_WIDTH), BF16),
        grid=(b, pairs, n_kb),
        in_specs=[pl.BlockSpec((None, MOBA_BLOCK, LANES), lambda bi, p, qi: (bi, qi, q_blk + p)),
                  pl.BlockSpec((None, seq, LANES), lambda bi, p, qi: (bi, 0, k_blk + p)),
                  pl.BlockSpec((None, seq, LANES), lambda bi, p, qi: (bi, 0, v_blk + p))],
        out_specs=pl.BlockSpec((None, MOBA_BLOCK, LANES), lambda bi, p, qi: (bi, qi, p)),
        scratch_shapes=[pltpu.VMEM((n_kb, LANES), F32),
                        pltpu.VMEM((n_kb, LANES, MOBA_BLOCK), BF16),
                        pltpu.VMEM((LANES // HEAD_DIM, seq, LANES), BF16),
                        pltpu.VMEM((LANES // HEAD_DIM, n_kb + 1, MOBA_BLOCK, MOBA_BLOCK), F32)],
        compiler_params=_cparams(3),
        name="moba_attention",
    )(proj3, proj3, proj3)


def _local_kernel(p_ref, ph_ref, ch_ref, chh_ref, cb_ref, cc_ref, cch_ref, pw_ref, ps_ref, cw_ref,
                  yp_ref, yc_ref, *, tm, tiles_per_seq):
    i = pl.program_id(0)
    seq_tile = i % tiles_per_seq
    has_left = seq_tile > 0

    def with_halo(halo_ref, body_ref):
        halo = jnp.where(has_left, halo_ref[...].astype(F32), 0.0)
        return jnp.concatenate([halo, body_ref[...].astype(F32)], axis=0)

    x = with_halo(ph_ref, p_ref)
    pos = seq_tile * tm + lax.broadcasted_iota(I32, (tm, 1), 0)
    for g, win in enumerate(POOL_WINDOWS):
        gs = slice(g * POOL_GROUP, (g + 1) * POOL_GROUP)
        xg = x[:, gs]
        s, k = xg, 1
        while k < win:
            s = s[k:] + s[:-k]
            k *= 2
        start = POOL_HALO + 1 - win
        window_sum = s[start:start + tm]
        count = jnp.minimum(pos + 1, win).astype(F32)
        mixed = window_sum / count - xg[POOL_HALO:]
        y = _dot(mixed.astype(BF16), pw_ref[g].astype(BF16)) * ps_ref[:, gs]
        yp_ref[:, gs] = y.astype(yp_ref.dtype)

    z = with_halo(chh_ref, ch_ref) * with_halo(cch_ref, cc_ref)
    y = jnp.zeros((tm, z.shape[1]), F32)
    for k in range(CONV_K):
        off = POOL_HALO - (CONV_K - 1) + k
        y = y + cw_ref[k:k + 1, :] * z[off:off + tm]
    yc_ref[...] = (cb_ref[...].astype(F32) * y).astype(yc_ref.dtype)


def _local_mixers(proj, pool_w, pool_scale, conv_w, seq, pool_blk, ch_blk, cb_blk, cc_blk, tm=512):
    n = proj.shape[0]
    w = BRANCH_WIDTH
    tm = min(tm, seq)
    tps = seq // tm
    hpt = tm // POOL_HALO

    def body(blk):
        return pl.BlockSpec((tm, w), lambda i: (i, blk))

    def halo(blk):
        return pl.BlockSpec((POOL_HALO, w), lambda i: (jnp.maximum(i * hpt - 1, 0), blk))

    kern = functools.partial(_local_kernel, tm=tm, tiles_per_seq=tps)
    return pl.pallas_call(
        kern,
        out_shape=(jax.ShapeDtypeStruct((n, w), BF16), jax.ShapeDtypeStruct((n, w), BF16)),
        grid=(n // tm,),
        in_specs=[body(pool_blk), halo(pool_blk), body(ch_blk), halo(ch_blk), body(cb_blk),
                  body(cc_blk), halo(cc_blk),
                  pl.BlockSpec(pool_w.shape, lambda i: (0, 0, 0)),
                  pl.BlockSpec((1, w), lambda i: (0, 0)),
                  pl.BlockSpec((CONV_K, w), lambda i: (0, 0))],
        out_specs=(pl.BlockSpec((tm, w), lambda i: (i, 0)), pl.BlockSpec((tm, w), lambda i: (i, 0))),
        compiler_params=_cparams(1),
        name="pool_conv",
    )(proj, proj, proj, proj, proj, proj, proj, pool_w, pool_scale.reshape(1, w), conv_w)


def _merge_kernel(b0, b1, b2, b3, g0, g1, g2, g3, wb_ref, gb_ref, o_ref, wbf_ref):
    @pl.when(pl.program_id(1) == 0)
    def _():
        wbf_ref[...] = wb_ref[...].astype(BF16)

    merged = None
    for nb, (br, gr) in enumerate(((b0, g0), (b1, g1), (b2, g2), (b3, g3))):
        lifted = _dot(br[...], wbf_ref[nb])
        gate = jax.nn.sigmoid(gr[...].astype(F32) + gb_ref[nb])
        term = gate * lifted
        merged = term if merged is None else merged + term
    o_ref[...] = merged.astype(o_ref.dtype)


def _merge(branches, proj, w_branch, gate_b, layer, gate_col, tm=1024, tn=512):
    n = proj.shape[0]
    d = w_branch.shape[3]
    assert gate_col % tn == 0 and d % tn == 0
    g_specs = [pl.BlockSpec((tm, tn), functools.partial(lambda j, i, nb: (i, (gate_col + nb * d) // tn + j), nb=nb))
               for nb in range(N_BRANCH)]
    b_specs = [pl.BlockSpec((tm, BRANCH_WIDTH), lambda j, i: (i, 0))] * N_BRANCH
    return pl.pallas_call(
        _merge_kernel,
        out_shape=jax.ShapeDtypeStruct((n, d), BF16),
        grid=(d // tn, n // tm),
        in_specs=b_specs + g_specs + [
            pl.BlockSpec((None, N_BRANCH, BRANCH_WIDTH, tn), lambda j, i: (layer, 0, 0, j)),
            pl.BlockSpec((N_BRANCH, 1, tn), lambda j, i: (0, 0, j))],
        out_specs=pl.BlockSpec((tm, tn), lambda j, i: (i, j)),
        scratch_shapes=[pltpu.VMEM((N_BRANCH, BRANCH_WIDTH, tn), BF16)],
        compiler_params=_cparams(2),
        name="gated_merge",
    )(*branches, proj, proj, proj, proj, w_branch, gate_b.reshape(N_BRANCH, 1, d))


def _post_norm(x, gate, y, g, b, alpha):
    h = alpha * x + (1.0 + gate) * y
    mu = jnp.mean(h, axis=1, keepdims=True)
    hc = h - mu
    var = jnp.mean(hc * hc, axis=1, keepdims=True)
    return hc * lax.rsqrt(var + LN_EPS) * g + b


def _outproj_kernel(mg_ref, w_ref, x_ref, gate_ref, g_ref, b_ref, sh_ref, sc_ref, rw_ref,
                    xo_ref, uo_ref, lg_ref, w_bf, *, alpha):
    @pl.when(pl.program_id(0) == 0)
    def _():
        w_bf[...] = w_ref[...].astype(BF16)

    y = _dot(mg_ref[...], w_bf[...])
    xn = _post_norm(x_ref[...], gate_ref[...], y, g_ref[...], b_ref[...], alpha)
    xo_ref[...] = xn
    u = xn * (1.0 + sc_ref[...]) + sh_ref[...]
    uo_ref[...] = u
    u_hi, u_lo = _split_bf16(u)
    w_hi, w_lo = _split_bf16(rw_ref[...])
    lg_ref[...] = _dot(u_hi, w_hi) + _dot(u_lo, w_hi) + _dot(u_hi, w_lo)


def _outproj_norm(merged, w_out, layer, x2, mod_l, ln_g, ln_b, router_w, alpha, seq, tm=256):
    n, d = x2.shape
    e = router_w.shape[2]
    tm = min(tm, seq)
    tps = seq // tm
    kern = functools.partial(_outproj_kernel, alpha=alpha)
    vec = lambda: pl.BlockSpec((None, 1, d), lambda i: (layer, 0, 0))
    row = lambda: pl.BlockSpec((tm, d), lambda i: (i, 0))
    once = pl.Buffered(1)
    return pl.pallas_call(
        kern,
        out_shape=(jax.ShapeDtypeStruct((n, d), F32), jax.ShapeDtypeStruct((n, d), F32),
                   jax.ShapeDtypeStruct((n, e), F32)),
        grid=(n // tm,),
        in_specs=[row(),
                  pl.BlockSpec((None, d, d), lambda i: (layer, 0, 0), pipeline_mode=once),
                  row(),
                  _mod_spec(2, d, tps), vec(), vec(), _mod_spec(3, d, tps), _mod_spec(4, d, tps),
                  pl.BlockSpec((None, d, e), lambda i: (layer, 0, 0), pipeline_mode=once)],
        out_specs=(row(), row(), pl.BlockSpec((tm, e), lambda i: (i, 0))),
        scratch_shapes=[pltpu.VMEM((d, d), BF16)],
        compiler_params=_cparams(1),
        name="out_proj_norm",
    )(merged, w_out, x2, mod_l, ln_g.reshape(-1, 1, d), ln_b.reshape(-1, 1, d), mod_l, mod_l, router_w)


def _route_kernel(lg_ref, bias_ref, idx_ref, w_ref, pos_ref, cnt_ref, carry_ref, *, tm):
    i = pl.program_id(0)

    @pl.when(i == 0)
    def _():
        carry_ref[...] = jnp.zeros_like(carry_ref)

    aff = jax.nn.sigmoid(lg_ref[...].T)
    e = aff.shape[0]
    sub = lax.broadcasted_iota(I32, (e, tm), 0).astype(F32)
    work = aff + bias_ref[...]
    picks = []
    chosen_f = jnp.zeros((e, tm), F32)
    for _ in range(TOP_K):
        best = jnp.max(work, axis=0, keepdims=True)
        pick = jnp.min(jnp.where(work == best, sub, float(e)), axis=0, keepdims=True)
        onehot = sub == pick
        picks.append((pick, onehot))
        chosen_f = jnp.where(onehot, 1.0, chosen_f)
        work = jnp.where(onehot, NEG_INF, work)

    sel_aff = chosen_f * aff
    dense_w = sel_aff / jnp.sum(sel_aff, axis=0, keepdims=True) * ROUTED_SCALE
    r = lax.broadcasted_iota(I32, (tm, tm), 0)
    c = lax.broadcasted_iota(I32, (tm, tm), 1)
    earlier = (r < c).astype(BF16)
    rank = _dot(chosen_f.astype(BF16), earlier) + carry_ref[...]
    carry_ref[...] += jnp.sum(chosen_f, axis=1, keepdims=True)
    cnt_ref[...] = carry_ref[...].astype(I32)

    slot = lax.broadcasted_iota(I32, (TOP_K, tm), 0)
    idx8 = jnp.zeros((TOP_K, tm), F32)
    w8 = jnp.zeros((TOP_K, tm), F32)
    pos8 = jnp.zeros((TOP_K, tm), F32)
    for kk, (pick, onehot) in enumerate(picks):
        idx8 = jnp.where(slot == kk, pick, idx8)
        w8 = jnp.where(slot == kk, jnp.sum(jnp.where(onehot, dense_w, 0.0), axis=0, keepdims=True), w8)
        pos8 = jnp.where(slot == kk, jnp.sum(jnp.where(onehot, rank, 0.0), axis=0, keepdims=True), pos8)
    idx_ref[...] = idx8.astype(I32)
    w_ref[...] = w8
    pos_ref[...] = pos8.astype(I32)


def _route(logits, router_bias, layer, tm=512):
    n, e = logits.shape
    kern = functools.partial(_route_kernel, tm=tm)
    slots = lambda: pl.BlockSpec((TOP_K, tm), lambda i: (0, i))
    return pl.pallas_call(
        kern,
        out_shape=(jax.ShapeDtypeStruct((TOP_K, n), I32), jax.ShapeDtypeStruct((TOP_K, n), F32),
                   jax.ShapeDtypeStruct((TOP_K, n), I32), jax.ShapeDtypeStruct((e, 1), I32)),
        grid=(n // tm,),
        in_specs=[pl.BlockSpec((tm, e), lambda i: (i, 0)),
                  pl.BlockSpec((None, e, 1), lambda i: (layer, 0, 0))],
        out_specs=(slots(), slots(), slots(), pl.BlockSpec((e, 1), lambda i: (0, 0))),
        scratch_shapes=[pltpu.VMEM((e, 1), F32)],
        compiler_params=_cparams(1),
        name="route",
    )(logits, router_bias.reshape(-1, e, 1))


def _dispatch_kernel(cnt_ref, start_ref, nu_ref, u_ref, idx_ref, pos_ref, dest_ref, xs_hbm,
                     dest_smem, zrow_ref, zblk_ref, idx_sem, row_sem, zero_sem,
                     *, tm, tr, n_tiles, n_blocks, n_experts):
    i = pl.program_id(0)
    idx = idx_ref[...]
    dest = pos_ref[...]
    for e in range(n_experts):
        dest = dest + jnp.where(idx == e, start_ref[e], 0)
    dest_ref[...] = dest
    idx_cp = pltpu.make_async_copy(dest_ref, dest_smem, idx_sem)
    idx_cp.start()
    idx_cp.wait()

    def issue(g, carry):
        base = pl.multiple_of(g * 8, 8)
        for u in range(8):
            for kk in range(TOP_K):
                dst = dest_smem[kk, base + u]
                pltpu.make_async_copy(u_ref.at[pl.ds(base + u, 1)], xs_hbm.at[pl.ds(dst, 1)],
                                      row_sem).start(priority=kk % 2)
        return carry

    lax.fori_loop(0, tm // 8, issue, 0)
    for kk in range(TOP_K):
        pltpu.make_async_copy(u_ref, xs_hbm.at[pl.ds(0, tm)], row_sem).wait()

    @pl.when(i == n_tiles - 1)
    def _():
        zrow_ref[...] = jnp.zeros_like(zrow_ref)
        zblk_ref[...] = jnp.zeros_like(zblk_ref)

        def pad_expert(e, carry):
            cnt = cnt_ref[e]
            first = start_ref[e] + cnt
            n_pad = (tr - cnt % tr) % tr

            def start(r, c):
                pltpu.make_async_copy(zrow_ref.at[pl.ds(0, 1)], xs_hbm.at[pl.ds(first + r, 1)], zero_sem).start()
                return c

            def wait(r, c):
                pltpu.make_async_copy(zrow_ref.at[pl.ds(0, 1)], xs_hbm.at[pl.ds(0, 1)], zero_sem).wait()
                return c

            lax.fori_loop(0, n_pad, start, 0)
            lax.fori_loop(0, n_pad, wait, 0)
            return carry

        lax.fori_loop(0, n_experts, pad_expert, 0)

        def tail_start(blk, c):
            row0 = pl.multiple_of(blk * tr, tr)
            pltpu.make_async_copy(zblk_ref, xs_hbm.at[pl.ds(row0, tr)], zero_sem).start()
            return c

        def tail_wait(blk, c):
            pltpu.make_async_copy(zblk_ref, xs_hbm.at[pl.ds(0, tr)], zero_sem).wait()
            return c

        lax.fori_loop(nu_ref[0], n_blocks, tail_start, 0)
        lax.fori_loop(nu_ref[0], n_blocks, tail_wait, 0)


def _dispatch(u2, idx_t, pos_t, counts, start_padded, n_used, tr, n_blocks, tm=512):
    n, d = u2.shape
    n_tiles = n // tm
    kern = functools.partial(_dispatch_kernel, tm=tm, tr=tr, n_tiles=n_tiles, n_blocks=n_blocks,
                             n_experts=counts.shape[0])
    slots = lambda: pl.BlockSpec((TOP_K, tm), lambda i, *_: (0, i))
    grid_spec = pltpu.PrefetchScalarGridSpec(
        num_scalar_prefetch=3,
        grid=(n_tiles,),
        in_specs=[pl.BlockSpec((tm, d), lambda i, *_: (i, 0)), slots(), slots()],
        out_specs=(slots(), pl.BlockSpec(memory_space=pl.ANY)),
        scratch_shapes=[pltpu.SMEM((TOP_K, tm), I32), pltpu.VMEM((8, d), F32), pltpu.VMEM((tr, d), F32),
                        pltpu.SemaphoreType.DMA(()), pltpu.SemaphoreType.DMA(()), pltpu.SemaphoreType.DMA(())])
    dest_t, xs = pl.pallas_call(
        kern,
        out_shape=(jax.ShapeDtypeStruct((TOP_K, n), I32), jax.ShapeDtypeStruct((n_blocks * tr, d), F32)),
        grid_spec=grid_spec,
        compiler_params=_cparams(1),
        name="dispatch",
    )(counts, start_padded, n_used, u2, idx_t, pos_t)
    return xs, dest_t


def _experts_kernel(be_ref, bs_ref, nxt_ref, nu_ref, x_ref, wg_hbm, wu_hbm, wd_hbm, y_ref,
                    wg_buf, wu_buf, wd_buf, wgu_bf, wd_bf, h_buf, state_ref, sems, *, hid, layer, n_blocks):
    i = pl.program_id(0)
    n_used = nu_ref[0]
    blk = jnp.minimum(i, n_blocks - 1)

    def fetch(e, s):
        return (pltpu.make_async_copy(wg_hbm.at[layer, e], wg_buf.at[s], sems.at[s, 0]),
                pltpu.make_async_copy(wu_hbm.at[layer, e], wu_buf.at[s], sems.at[s, 1]),
                pltpu.make_async_copy(wd_hbm.at[layer, e], wd_buf.at[s], sems.at[s, 2]))

    @pl.when(i == 0)
    def _():
        state_ref[0] = 0
        state_ref[1] = 0
        for cp in fetch(be_ref[0], 0):
            cp.start()

    down_slot = state_ref[1]
    e = be_ref[blk]
    changed = jnp.logical_and(i < n_used, jnp.logical_or(i == 0, e != be_ref[jnp.maximum(blk - 1, 0)]))

    @pl.when(changed)
    def _():
        s = state_ref[0]
        for cp in fetch(e, s):
            cp.wait()
        nxt = nxt_ref[e]

        @pl.when(nxt >= 0)
        def _():
            for cp in fetch(nxt, 1 - s):
                cp.start()

        wgu_bf[:, :hid] = wg_buf[s].astype(BF16)
        wgu_bf[:, hid:] = wu_buf[s].astype(BF16)
        wd_bf[1 - down_slot] = wd_buf[s].astype(BF16)
        state_ref[0] = 1 - s
        state_ref[1] = 1 - down_slot

    def gate_up():
        gu = _dot(x_ref[...].astype(BF16), wgu_bf[...])
        g = gu[:, :hid]
        h_buf[i % 2] = ((g * jax.nn.sigmoid(g)) * gu[:, hid:]).astype(BF16)

    def down():
        y_ref[...] = _dot(h_buf[(i + 1) % 2], wd_bf[down_slot])

    @pl.when(jnp.logical_and(i >= 1, i < n_used))
    def _():
        down()
        gate_up()

    @pl.when(i == 0)
    def _():
        gate_up()

    @pl.when(i == n_used)
    def _():
        down()

    @pl.when(i > n_used)
    def _():
        y_ref[...] = jnp.zeros_like(y_ref)


def _experts(xs, block_e, block_src, next_e, n_used, w_gate, w_up, w_down, layer, tr):
    rows, d = xs.shape
    n_blocks = rows // tr
    hid = w_gate.shape[3]
    kern = functools.partial(_experts_kernel, hid=hid, layer=layer, n_blocks=n_blocks)
    hbm = lambda: pl.BlockSpec(memory_space=pl.ANY)
    grid_spec = pltpu.PrefetchScalarGridSpec(
        num_scalar_prefetch=4,
        grid=(n_blocks + 1,),
        in_specs=[pl.BlockSpec((tr, d), lambda i, be, bs, nx, nu: (bs[jnp.minimum(i, n_blocks - 1)], 0)),
                  hbm(), hbm(), hbm()],
        out_specs=pl.BlockSpec((tr, d), lambda i, be, bs, nx, nu: (jnp.maximum(i - 1, 0), 0)),
        scratch_shapes=[pltpu.VMEM((2, d, hid), F32), pltpu.VMEM((2, d, hid), F32), pltpu.VMEM((2, hid, d), F32),
                        pltpu.VMEM((d, 2 * hid), BF16), pltpu.VMEM((2, hid, d), BF16),
                        pltpu.VMEM((2, tr, hid), BF16),
                        pltpu.SMEM((2,), I32), pltpu.SemaphoreType.DMA((2, 3))])
    return pl.pallas_call(
        kern,
        out_shape=jax.ShapeDtypeStruct((rows, d), F32),
        grid_spec=grid_spec,
        compiler_params=_cparams(1),
        name="routed_experts",
    )(block_e, block_src, next_e, n_used, xs, w_gate, w_up, w_down)


def _shared_kernel(u_ref, wg_ref, wu_ref, wd_ref, o_ref, wg_bf, wu_bf, wd_bf):
    @pl.when(pl.program_id(0) == 0)
    def _():
        wg_bf[...] = wg_ref[...].astype(BF16)
        wu_bf[...] = wu_ref[...].astype(BF16)
        wd_bf[...] = wd_ref[...].astype(BF16)

    xb = u_ref[...].astype(BF16)
    g = _dot(xb, wg_bf[...])
    h = (g * jax.nn.sigmoid(g)) * _dot(xb, wu_bf[...])
    o_ref[...] = _dot(h.astype(BF16), wd_bf[...])


def _shared_expert(u2, sh_gate, sh_up, sh_down, layer, tm=512):
    n, d = u2.shape
    hid = sh_gate.shape[2]
    return pl.pallas_call(
        _shared_kernel,
        out_shape=jax.ShapeDtypeStruct((n, d), F32),
        grid=(n // tm,),
        in_specs=[pl.BlockSpec((tm, d), lambda i: (i, 0)),
                  pl.BlockSpec((None, d, hid), lambda i: (layer, 0, 0)),
                  pl.BlockSpec((None, d, hid), lambda i: (layer, 0, 0)),
                  pl.BlockSpec((None, hid, d), lambda i: (layer, 0, 0))],
        out_specs=pl.BlockSpec((tm, d), lambda i: (i, 0)),
        scratch_shapes=[pltpu.VMEM((d, hid), BF16), pltpu.VMEM((d, hid), BF16), pltpu.VMEM((hid, d), BF16)],
        compiler_params=_cparams(1),
        name="shared_expert",
    )(u2, sh_gate, sh_up, sh_down)


def _combine_kernel(dest_ref, y_hbm, w8_ref, shd_ref, x_ref, gate_ref, g_ref, b_ref, sh_ref, sc_ref,
                    xo_ref, uo_ref, idx_smem, ybuf, idx_sem, row_sem, *, tm, n_tiles, alpha):
    i = pl.program_id(0)
    slot = i % 2

    def gather(tile, s):
        cp = pltpu.make_async_copy(dest_ref.at[tile], idx_smem.at[s], idx_sem)
        cp.start()
        cp.wait()

        def issue(g, carry):
            base = pl.multiple_of(g * 8, 8)
            for u in range(8):
                for kk in range(TOP_K):
                    src = idx_smem[s, kk, base + u]
                    pltpu.make_async_copy(y_hbm.at[pl.ds(src, 1)], ybuf.at[s, kk, pl.ds(base + u, 1)],
                                          row_sem.at[s]).start(priority=kk % 2)
            return carry

        lax.fori_loop(0, tm // 8, issue, 0)

    @pl.when(i == 0)
    def _():
        gather(0, 0)

    @pl.when(i + 1 < n_tiles)
    def _():
        gather(i + 1, 1 - slot)

    for kk in range(TOP_K):
        pltpu.make_async_copy(y_hbm.at[pl.ds(0, tm)], ybuf.at[slot, kk], row_sem.at[slot]).wait()
    w8 = w8_ref[...]
    ffn = shd_ref[...]
    for kk in range(TOP_K):
        ffn = ffn + w8[:, kk:kk + 1] * ybuf[slot, kk]
    xn = _post_norm(x_ref[...], gate_ref[...], ffn, g_ref[...], b_ref[...], alpha)
    xo_ref[...] = xn
    uo_ref[...] = (xn * (1.0 + sc_ref[...]) + sh_ref[...]).astype(uo_ref.dtype)


def _combine_norm(dest_t, y_sorted, w8, shared, x2, mod_l, mod_next, ln_g, ln_b, layer, alpha, seq, tm=128):
    n, d = x2.shape
    n_tiles = n // tm
    tm = min(tm, seq)
    tps = seq // tm
    kern = functools.partial(_combine_kernel, tm=tm, n_tiles=n_tiles, alpha=alpha)
    row = lambda: pl.BlockSpec((tm, d), lambda i: (i, 0))
    vec = lambda: pl.BlockSpec((None, 1, d), lambda i: (layer, 0, 0))
    return pl.pallas_call(
        kern,
        out_shape=(jax.ShapeDtypeStruct((n, d), F32), jax.ShapeDtypeStruct((n, d), BF16)),
        grid=(n_tiles,),
        in_specs=[pl.BlockSpec((n_tiles, TOP_K, tm), lambda i: (0, 0, 0)),
                  pl.BlockSpec(memory_space=pl.ANY),
                  pl.BlockSpec((tm, TOP_K), lambda i: (i, 0)),
                  row(), row(), _mod_spec(5, d, tps), vec(), vec(),
                  _mod_spec(0, d, tps), _mod_spec(1, d, tps)],
        out_specs=(row(), row()),
        scratch_shapes=[pltpu.SMEM((2, TOP_K, tm), I32), pltpu.VMEM((2, TOP_K, tm, d), F32),
                        pltpu.SemaphoreType.DMA(()), pltpu.SemaphoreType.DMA((2,))],
        compiler_params=_cparams(1),
        name="combine_norm",
    )(dest_t.reshape(TOP_K, n_tiles, tm).transpose(1, 0, 2), y_sorted, w8, shared, x2, mod_l,
      ln_g.reshape(-1, 1, d), ln_b.reshape(-1, 1, d), mod_next, mod_next)


def _dispatch_plan(counts, tr, n_blocks):
    padded = (counts + tr - 1) // tr * tr
    end_padded = jnp.cumsum(padded)
    start_padded = (end_padded - padded).astype(I32)
    n_used = (end_padded[-1] // tr).astype(I32)
    block = jnp.arange(n_blocks, dtype=I32)
    block_src = jnp.minimum(block, n_used - 1)
    block_e = jnp.sum((block_src[:, None] * tr >= end_padded[None, :]).astype(I32), axis=1)
    n_e = counts.shape[0]
    ids = jnp.where(counts > 0, jnp.arange(n_e, dtype=I32), n_e)
    later = jnp.concatenate([lax.cummin(ids[::-1])[::-1][1:], jnp.full((1,), n_e, I32)])
    next_e = jnp.where(later < n_e, later, -1).astype(I32)
    return start_padded, block_e.astype(I32), block_src, next_e, n_used.reshape(1)


def kernel(x, c, ada_w, ada_b, w_in, gate_b, pool_w, pool_scale, conv_w, w_branch, w_out, ln1_g, ln1_b,
           router_w, router_bias, exp_w_gate, exp_w_up, exp_w_down, sh_w_gate, sh_w_up, sh_w_down, ln2_g, ln2_b):
    b, seq, d = x.shape
    depth = ada_w.shape[0]
    n = b * seq
    alpha = (2 * depth) ** 0.25
    expert_rows = 256
    n_blocks = -(-(n * TOP_K + N_EXPERTS * (expert_rows - 1)) // expert_rows)

    w = BRANCH_WIDTH
    cb = w // LANES
    sq, sk, sv = 0, cb, 2 * cb
    pool_col = 3 * w
    mq_col = 4 * w
    mq, mk, mv = mq_col // LANES, mq_col // LANES + cb, mq_col // LANES + 2 * cb
    conv_col = 7 * w
    gate_col = 10 * w

    mod = _ada(c, ada_w, ada_b)
    cos_t, sin_t = _rope_tables(seq)
    x2 = x.reshape(n, d)
    u = _modulate(x2, mod[0], seq)
    for l in range(depth):
        proj = _inproj(u, w_in, l, cos_t, sin_t, seq, rot_col=mq_col)
        proj3 = proj.reshape(b, seq, -1)
        y_sb = _sb_attention(proj3, sq, sk, sv).reshape(n, w)
        y_mb = _moba_attention(proj3, mq, mk, mv).reshape(n, w)
        y_pool, y_cv = _local_mixers(proj, pool_w[l], pool_scale[l], conv_w[l], seq,
                                     pool_col // w, conv_col // w, conv_col // w + 1, conv_col // w + 2)
        merged = _merge((y_sb, y_pool, y_mb, y_cv), proj, w_branch, gate_b[l], l, gate_col)
        x2, u2, logits = _outproj_norm(merged, w_out, l, x2, mod[l], ln1_g, ln1_b, router_w, alpha, seq)
        idx_t, w_t, pos_t, counts = _route(logits, router_bias, l)
        counts = counts[:, 0]
        start_padded, block_e, block_src, next_e, n_used = _dispatch_plan(counts, expert_rows, n_blocks)
        xs, dest_t = _dispatch(u2, idx_t, pos_t, counts, start_padded, n_used, expert_rows, n_blocks)
        y_sorted = _experts(xs, block_e, block_src, next_e, n_used, exp_w_gate, exp_w_up, exp_w_down, l,
                            expert_rows)
        shared = _shared_expert(u2, sh_w_gate, sh_w_up, sh_w_down, l)
        x2, u = _combine_norm(dest_t, y_sorted, w_t.T, shared, x2, mod[l], mod[(l + 1) % depth],
                              ln2_g, ln2_b, l, alpha, seq)
    return x2.reshape(b, seq, d)
```

```python
import functools

import jax
import jax.numpy as jnp
from jax import lax
from jax.experimental import pallas as pl
from jax.experimental.pallas import tpu as pltpu

F32 = jnp.float32
BF16 = jnp.bfloat16
I32 = jnp.int32

HEAD_DIM = 64
N_BRANCH = 4
BRANCH_WIDTH = 512
POOL_WINDOWS = (2, 4, 8, 16)
POOL_GROUP = 128
POOL_HALO = 16
CONV_K = 3
MOBA_BLOCK = 256
MOBA_TOPK = 3
ROPE_THETA = 10000.0
N_EXPERTS = 64
TOP_K = 8
EXPERT_HIDDEN = 384
ROUTED_SCALE = 2.5
LN_EPS = 1e-5
NEG_INF = float("-inf")
SB_DEAD_LOG = -100.0
MASKED = -1e30

LANES = 128
VMEM_LIMIT = 56 * 1024 * 1024


def _cparams(n_axes, vmem=VMEM_LIMIT):
    return pltpu.CompilerParams(dimension_semantics=("arbitrary",) * n_axes, vmem_limit_bytes=vmem)


def _dot(a, b):
    return jnp.dot(a, b, preferred_element_type=F32)


def _dot_nt(a, b):
    return lax.dot_general(a, b, (((1,), (1,)), ((), ())), preferred_element_type=F32)


def _split_bf16(x):
    hi = x.astype(BF16)
    lo = (x - hi.astype(F32)).astype(BF16)
    return hi, lo


def _ada_kernel(c_ref, w_ref, b_ref, o_ref):
    c = c_ref[...]
    cond = c * jax.nn.sigmoid(c)
    o_ref[...] = jnp.dot(cond, w_ref[...], preferred_element_type=F32,
                         precision=lax.Precision.HIGHEST) + b_ref[...]


def _ada(c, ada_w, ada_b):
    depth, d, d6 = ada_w.shape
    b = c.shape[0]
    rows = 8
    tn = 1024
    c8 = jnp.zeros((rows, d), F32).at[:b].set(c)
    out = pl.pallas_call(
        _ada_kernel,
        out_shape=jax.ShapeDtypeStruct((depth, rows, d6), F32),
        grid=(depth, d6 // tn),
        in_specs=[pl.BlockSpec((rows, d), lambda l, n: (0, 0)),
                  pl.BlockSpec((None, d, tn), lambda l, n: (l, 0, n)),
                  pl.BlockSpec((None, 1, tn), lambda l, n: (l, 0, n))],
        out_specs=pl.BlockSpec((None, rows, tn), lambda l, n: (l, 0, n)),
        compiler_params=_cparams(2),
        name="ada_mod",
    )(c8, ada_w, ada_b.reshape(depth, 1, d6))
    return out[:, :b].reshape(depth, b, 6, 1, d)


def _mod_spec(which, d, tiles_per_seq):
    return pl.BlockSpec((None, None, 1, d), lambda i, *_: (i // tiles_per_seq, which, 0, 0))


def _modulate_kernel(x_ref, sh_ref, sc_ref, u_ref):
    u_ref[...] = (x_ref[...] * (1.0 + sc_ref[...]) + sh_ref[...]).astype(u_ref.dtype)


def _modulate(x2, mod_l, seq, tm=512):
    n, d = x2.shape
    tm = min(tm, seq)
    tps = seq // tm
    return pl.pallas_call(
        _modulate_kernel,
        out_shape=jax.ShapeDtypeStruct((n, d), BF16),
        grid=(n // tm,),
        in_specs=[pl.BlockSpec((tm, d), lambda i: (i, 0)), _mod_spec(0, d, tps), _mod_spec(1, d, tps)],
        out_specs=pl.BlockSpec((tm, d), lambda i: (i, 0)),
        compiler_params=_cparams(1),
        name="modulate",
    )(x2, mod_l, mod_l)


def _inproj_kernel(u_ref, w_ref, cos_ref, sin_ref, o_ref, wbf_ref, *, rot_tile, tn):
    n = pl.program_id(0)
    m = pl.program_id(1)

    @pl.when(m == 0)
    def _():
        wbf_ref[...] = w_ref[...].astype(BF16)

    acc = _dot(u_ref[...], wbf_ref[...])

    @pl.when(n == rot_tile)
    def _():
        reps = tn // LANES
        cos = jnp.concatenate([cos_ref[...]] * reps, axis=1)
        sin = jnp.concatenate([sin_ref[...]] * reps, axis=1)
        lane = lax.broadcasted_iota(I32, acc.shape, 1)
        first_half = (lane % HEAD_DIM) < (HEAD_DIM // 2)
        half = HEAD_DIM // 2
        partner = jnp.where(first_half, pltpu.roll(acc, tn - half, 1), pltpu.roll(acc, half, 1))
        o_ref[...] = (acc * cos + partner * sin).astype(o_ref.dtype)

    @pl.when(n != rot_tile)
    def _():
        o_ref[...] = acc.astype(o_ref.dtype)


def _inproj(u, w_in, layer, cos_t, sin_t, seq, rot_col, tm=1024, tn=1024):
    n, d = u.shape
    cols = w_in.shape[2]
    tm = min(tm, seq)
    tps = seq // tm
    assert rot_col % tn == 0
    kern = functools.partial(_inproj_kernel, rot_tile=rot_col // tn, tn=tn)
    return pl.pallas_call(
        kern,
        out_shape=jax.ShapeDtypeStruct((n, cols), BF16),
        grid=(cols // tn, n // tm),
        in_specs=[pl.BlockSpec((tm, d), lambda j, i: (i, 0)),
                  pl.BlockSpec((None, d, tn), lambda j, i: (layer, 0, j)),
                  pl.BlockSpec((tm, LANES), lambda j, i: (i % tps, 0)),
                  pl.BlockSpec((tm, LANES), lambda j, i: (i % tps, 0))],
        out_specs=pl.BlockSpec((tm, tn), lambda j, i: (i, j)),
        scratch_shapes=[pltpu.VMEM((d, tn), BF16)],
        compiler_params=_cparams(2),
        name="in_proj",
    )(u, w_in, cos_t, sin_t)


def _rope_tables(seq):
    half = HEAD_DIM // 2
    inv_freq = ROPE_THETA ** (-jnp.arange(half, dtype=F32) / half)
    ang = jnp.arange(seq, dtype=F32)[:, None] * inv_freq[None, :]
    cos, sin = jnp.cos(ang), jnp.sin(ang)
    cos_h = jnp.concatenate([cos, cos], axis=1)
    sin_h = jnp.concatenate([-sin, sin], axis=1)
    reps = LANES // HEAD_DIM
    return jnp.tile(cos_h, (1, reps)), jnp.tile(sin_h, (1, reps))


def _sb_kernel(q_ref, k_ref, v_ref, o_ref, *, tq, tk):
    qi = pl.program_id(2)
    q0 = qi * tq
    r = lax.broadcasted_iota(I32, (tk, tk), 0)
    c = lax.broadcasted_iota(I32, (tk, tk), 1)
    later = (r > c).astype(BF16)
    heads = LANES // HEAD_DIM
    qs = [q_ref[:, h * HEAD_DIM:(h + 1) * HEAD_DIM] * (HEAD_DIM ** -0.5) for h in range(heads)]

    def block(j0, h, carry, acc, masked):
        k = k_ref[pl.ds(j0, tk), h * HEAD_DIM:(h + 1) * HEAD_DIM]
        v = v_ref[pl.ds(j0, tk), h * HEAD_DIM:(h + 1) * HEAD_DIM]
        z = _dot_nt(qs[h], k)
        softplus = jnp.maximum(z, 0.0) + jnp.log(1.0 + jnp.exp(-jnp.abs(z)))
        log_keep = -softplus
        if masked:
            qpos = q0 + lax.broadcasted_iota(I32, (tq, tk), 0)
            kpos = j0 + lax.broadcasted_iota(I32, (tq, tk), 1)
            past = kpos < qpos
            log_keep = jnp.where(past, log_keep, 0.0)
        hi, lo = _split_bf16(log_keep)
        between = _dot(hi, later) + _dot(lo, later) + carry
        w = jnp.exp((z - softplus) + between)
        if masked:
            w = jnp.where(past, w, 0.0)
        acc = acc + _dot(w.astype(BF16), v)
        carry = carry + jnp.sum(log_keep, axis=1, keepdims=True)
        return carry, acc

    state = [(jnp.zeros((tq, 1), F32), jnp.zeros((tq, HEAD_DIM), F32)) for _ in range(heads)]
    n_diag = tq // tk
    for d in range(n_diag - 1, -1, -1):
        j0 = pl.multiple_of(q0 + d * tk, tk)
        state = [block(j0, h, *state[h], True) for h in range(heads)]

    def live(pairs):
        return functools.reduce(jnp.maximum, [jnp.max(c) for c, _ in pairs])

    def cond(st):
        return jnp.logical_and(st[0] < qi, st[1] > SB_DEAD_LOG)

    def body(st):
        jj = st[0]
        new = [(st[2 + 2 * h], st[3 + 2 * h]) for h in range(heads)]
        for part in range(n_diag):
            j0 = pl.multiple_of(q0 - (jj * n_diag + part + 1) * tk, tk)
            new = [block(j0, h, *new[h], False) for h in range(heads)]
        return (jj + 1, live(new)) + tuple(x for pair in new for x in pair)

    final = lax.while_loop(cond, body, (jnp.int32(0), live(state)) + tuple(x for pair in state for x in pair))
    o_ref[...] = jnp.concatenate([final[3 + 2 * h] for h in range(heads)], axis=1).astype(o_ref.dtype)


def _sb_attention(proj3, q_blk, k_blk, v_blk, tq=256, tk=128):
    b, seq, _ = proj3.shape
    pairs = BRANCH_WIDTH // LANES
    kern = functools.partial(_sb_kernel, tq=tq, tk=tk)
    return pl.pallas_call(
        kern,
        out_shape=jax.ShapeDtypeStruct((b, seq, BRANCH_WIDTH), BF16),
        grid=(b, pairs, seq // tq),
        in_specs=[pl.BlockSpec((None, tq, LANES), lambda bi, p, qi: (bi, qi, q_blk + p)),
                  pl.BlockSpec((None, seq, LANES), lambda bi, p, qi: (bi, 0, k_blk + p)),
                  pl.BlockSpec((None, seq, LANES), lambda bi, p, qi: (bi, 0, v_blk + p))],
        out_specs=pl.BlockSpec((None, tq, LANES), lambda bi, p, qi: (bi, qi, p)),
        compiler_params=_cparams(3),
        name="sb_attention",
    )(proj3, proj3, proj3)


def _moba_kernel(q_ref, k_ref, v_ref, o_ref, kmean_ref, kt_ref, vaug_ref, s_ref, *, n_kb):
    own = pl.program_id(2)
    tq = MOBA_BLOCK
    heads = LANES // HEAD_DIM
    last = n_kb - 1

    @pl.when(own == 0)
    def _():
        s_ref[...] = jnp.zeros_like(s_ref)
        ones = jnp.ones((MOBA_BLOCK, HEAD_DIM), BF16)
        for jb in range(n_kb):
            rows = slice(jb * MOBA_BLOCK, (jb + 1) * MOBA_BLOCK)
            kb = k_ref[rows, :]
            kmean_ref[jb:jb + 1, :] = jnp.mean(kb.astype(F32), axis=0, keepdims=True)
            kt_ref[jb] = kb.T
            vb = v_ref[rows, :]
            for h in range(heads):
                vaug_ref[h, rows, :] = jnp.concatenate([vb[:, h * HEAD_DIM:(h + 1) * HEAD_DIM], ones], axis=1)

    blk = lax.broadcasted_iota(I32, (n_kb, tq), 0)
    row = lax.broadcasted_iota(I32, (tq, MOBA_BLOCK), 0)
    col = lax.broadcasted_iota(I32, (tq, MOBA_BLOCK), 1)
    causal = col <= row
    block_id = lax.broadcasted_iota(I32, (n_kb, MOBA_BLOCK), 0)

    qs, sel_bias, run_max = [], [], []
    for h in range(heads):
        hs = slice(h * HEAD_DIM, (h + 1) * HEAD_DIM)
        q = q_ref[:, hs] * (HEAD_DIM ** -0.5)
        qs.append(q)
        km_hi, km_lo = _split_bf16(kmean_ref[:, hs])
        gate = _dot_nt(km_hi, q) + _dot_nt(km_lo, q)
        beaten = jnp.zeros((n_kb, tq), I32)
        for jp in range(n_kb):
            g = gate[jp:jp + 1, :]
            beats = (g > gate) | ((g == gate) & (jp < blk))
            beaten = beaten + jnp.where(beats, 1, 0) * (jp < own).astype(I32)
        chosen = (beaten < MOBA_TOPK) & (blk < own)
        sel_bias.append(jnp.where(chosen, 0.0, MASKED).T.astype(BF16))
        run_max.append(jnp.full((tq, MOBA_BLOCK), NEG_INF, F32))

    def score_pass(t, running):
        out = list(running)
        for half in range(2):
            j = 2 * t + half
            valid = j < own
            jc = jnp.minimum(j, last)
            pick = (block_id == jc).astype(BF16)
            for h in range(heads):
                hs = slice(h * HEAD_DIM, (h + 1) * HEAD_DIM)
                s = _dot(qs[h], kt_ref[jc, hs, :]) + _dot(sel_bias[h], pick)
                s_ref[h, jnp.where(valid, j, n_kb)] = s
                out[h] = jnp.where(valid, jnp.maximum(out[h], s), out[h])
        return tuple(out)

    run_max = lax.fori_loop(0, (own + 1) // 2, score_pass, tuple(run_max))
    row_max = []
    for h in range(heads):
        hs = slice(h * HEAD_DIM, (h + 1) * HEAD_DIM)
        s = jnp.where(causal, _dot(qs[h], kt_ref[own, hs, :]), NEG_INF)
        s_ref[h, own] = s
        row_max.append(jnp.max(jnp.maximum(run_max[h], s), axis=1, keepdims=True))

    def softmax_pass(t, accs):
        out = list(accs)
        for half in range(2):
            j = 2 * t + half
            valid = j <= own
            jc = jnp.minimum(j, last)
            j0 = pl.multiple_of(jc * MOBA_BLOCK, MOBA_BLOCK)
            for h in range(heads):
                p = jnp.where(valid, jnp.exp(s_ref[h, jc] - row_max[h]), 0.0)
                out[h] = out[h] + _dot(p.astype(BF16), vaug_ref[h, pl.ds(j0, MOBA_BLOCK), :])
        return tuple(out)

    accs = lax.fori_loop(0, (own + 2) // 2, softmax_pass, tuple(jnp.zeros((tq, LANES), F32) for _ in range(heads)))
    o_ref[...] = jnp.concatenate([a[:, :HEAD_DIM] / a[:, HEAD_DIM:] for a in accs], axis=1).astype(o_ref.dtype)


def _moba_attention(proj3, q_blk, k_blk, v_blk):
    b, seq, _ = proj3.shape
    assert seq % MOBA_BLOCK == 0
    n_kb = seq // MOBA_BLOCK
    pairs = BRANCH_WIDTH // LANES
    kern = functools.partial(_moba_kernel, n_kb=n_kb)
    return pl.pallas_call(
        kern,
        out_shape=jax.ShapeDtypeStruct((b, seq, BRANCH_WIDTH), BF16),
        grid=(b, pairs, n_kb),
        in_specs=[pl.BlockSpec((None, MOBA_BLOCK, LANES), lambda bi, p, qi: (bi, qi, q_blk + p)),
                  pl.BlockSpec((None, seq, LANES), lambda bi, p, qi: (bi, 0, k_blk + p)),
                  pl.BlockSpec((None, seq, LANES), lambda bi, p, qi: (bi, 0, v_blk + p))],
        out_specs=pl.BlockSpec((None, MOBA_BLOCK, LANES), lambda bi, p, qi: (bi, qi, p)),
        scratch_shapes=[pltpu.VMEM((n_kb, LANES), F32),
                        pltpu.VMEM((n_kb, LANES, MOBA_BLOCK), BF16),
                        pltpu.VMEM((LANES // HEAD_DIM, seq, LANES), BF16),
                        pltpu.VMEM((LANES // HEAD_DIM, n_kb + 1, MOBA_BLOCK, MOBA_BLOCK), F32)],
        compiler_params=_cparams(3),
        name="moba_attention",
    )(proj3, proj3, proj3)


def _local_kernel(p_ref, ph_ref, ch_ref, chh_ref, cb_ref, cc_ref, cch_ref, pw_ref, ps_ref, cw_ref,
                  yp_ref, yc_ref, *, tm, tiles_per_seq):
    i = pl.program_id(0)
    seq_tile = i % tiles_per_seq
    has_left = seq_tile > 0

    def with_halo(halo_ref, body_ref):
        halo = jnp.where(has_left, halo_ref[...].astype(F32), 0.0)
        return jnp.concatenate([halo, body_ref[...].astype(F32)], axis=0)

    x = with_halo(ph_ref, p_ref)
    pos = seq_tile * tm + lax.broadcasted_iota(I32, (tm, 1), 0)
    for g, win in enumerate(POOL_WINDOWS):
        gs = slice(g * POOL_GROUP, (g + 1) * POOL_GROUP)
        xg = x[:, gs]
        s, k = xg, 1
        while k < win:
            s = s[k:] + s[:-k]
            k *= 2
        start = POOL_HALO + 1 - win
        window_sum = s[start:start + tm]
        count = jnp.minimum(pos + 1, win).astype(F32)
        mixed = window_sum / count - xg[POOL_HALO:]
        y = _dot(mixed.astype(BF16), pw_ref[g].astype(BF16)) * ps_ref[:, gs]
        yp_ref[:, gs] = y.astype(yp_ref.dtype)

    z = with_halo(chh_ref, ch_ref) * with_halo(cch_ref, cc_ref)
    y = jnp.zeros((tm, z.shape[1]), F32)
    for k in range(CONV_K):
        off = POOL_HALO - (CONV_K - 1) + k
        y = y + cw_ref[k:k + 1, :] * z[off:off + tm]
    yc_ref[...] = (cb_ref[...].astype(F32) * y).astype(yc_ref.dtype)


def _local_mixers(proj, pool_w, pool_scale, conv_w, seq, pool_blk, ch_blk, cb_blk, cc_blk, tm=512):
    n = proj.shape[0]
    w = BRANCH_WIDTH
    tm = min(tm, seq)
    tps = seq // tm
    hpt = tm // POOL_HALO

    def body(blk):
        return pl.BlockSpec((tm, w), lambda i: (i, blk))

    def halo(blk):
        return pl.BlockSpec((POOL_HALO, w), lambda i: (jnp.maximum(i * hpt - 1, 0), blk))

    kern = functools.partial(_local_kernel, tm=tm, tiles_per_seq=tps)
    return pl.pallas_call(
        kern,
        out_shape=(jax.ShapeDtypeStruct((n, w), BF16), jax.ShapeDtypeStruct((n, w), BF16)),
        grid=(n // tm,),
        in_specs=[body(pool_blk), halo(pool_blk), body(ch_blk), halo(ch_blk), body(cb_blk),
                  body(cc_blk), halo(cc_blk),
                  pl.BlockSpec(pool_w.shape, lambda i: (0, 0, 0)),
                  pl.BlockSpec((1, w), lambda i: (0, 0)),
                  pl.BlockSpec((CONV_K, w), lambda i: (0, 0))],
        out_specs=(pl.BlockSpec((tm, w), lambda i: (i, 0)), pl.BlockSpec((tm, w), lambda i: (i, 0))),
        compiler_params=_cparams(1),
        name="pool_conv",
    )(proj, proj, proj, proj, proj, proj, proj, pool_w, pool_scale.reshape(1, w), conv_w)


def _merge_kernel(b0, b1, b2, b3, g0, g1, g2, g3, wb_ref, gb_ref, o_ref, wbf_ref):
    @pl.when(pl.program_id(1) == 0)
    def _():
        wbf_ref[...] = wb_ref[...].astype(BF16)

    merged = None
    for nb, (br, gr) in enumerate(((b0, g0), (b1, g1), (b2, g2), (b3, g3))):
        lifted = _dot(br[...], wbf_ref[nb])
        gate = jax.nn.sigmoid(gr[...].astype(F32) + gb_ref[nb])
        term = gate * lifted
        merged = term if merged is None else merged + term
    o_ref[...] = merged.astype(o_ref.dtype)


def _merge(branches, proj, w_branch, gate_b, layer, gate_col, tm=1024, tn=512):
    n = proj.shape[0]
    d = w_branch.shape[3]
    assert gate_col % tn == 0 and d % tn == 0
    g_specs = [pl.BlockSpec((tm, tn), functools.partial(lambda j, i, nb: (i, (gate_col + nb * d) // tn + j), nb=nb))
               for nb in range(N_BRANCH)]
    b_specs = [pl.BlockSpec((tm, BRANCH_WIDTH), lambda j, i: (i, 0))] * N_BRANCH
    return pl.pallas_call(
        _merge_kernel,
        out_shape=jax.ShapeDtypeStruct((n, d), BF16),
        grid=(d // tn, n // tm),
        in_specs=b_specs + g_specs + [
            pl.BlockSpec((None, N_BRANCH, BRANCH_WIDTH, tn), lambda j, i: (layer, 0, 0, j)),
            pl.BlockSpec((N_BRANCH, 1, tn), lambda j, i: (0, 0, j))],
        out_specs=pl.BlockSpec((tm, tn), lambda j, i: (i, j)),
        scratch_shapes=[pltpu.VMEM((N_BRANCH, BRANCH_WIDTH, tn), BF16)],
        compiler_params=_cparams(2),
        name="gated_merge",
    )(*branches, proj, proj, proj, proj, w_branch, gate_b.reshape(N_BRANCH, 1, d))


def _post_norm(x, gate, y, g, b, alpha):
    h = alpha * x + (1.0 + gate) * y
    mu = jnp.mean(h, axis=1, keepdims=True)
    hc = h - mu
    var = jnp.mean(hc * hc, axis=1, keepdims=True)
    return hc * lax.rsqrt(var + LN_EPS) * g + b


def _outproj_kernel(mg_ref, w_ref, x_ref, gate_ref, g_ref, b_ref, sh_ref, sc_ref, rw_ref,
                    xo_ref, uo_ref, lg_ref, w_bf, *, alpha):
    @pl.when(pl.program_id(0) == 0)
    def _():
        w_bf[...] = w_ref[...].astype(BF16)

    y = _dot(mg_ref[...], w_bf[...])
    xn = _post_norm(x_ref[...], gate_ref[...], y, g_ref[...], b_ref[...], alpha)
    xo_ref[...] = xn
    u = xn * (1.0 + sc_ref[...]) + sh_ref[...]
    uo_ref[...] = u
    u_hi, u_lo = _split_bf16(u)
    w_hi, w_lo = _split_bf16(rw_ref[...])
    lg_ref[...] = _dot(u_hi, w_hi) + _dot(u_lo, w_hi) + _dot(u_hi, w_lo)


def _outproj_norm(merged, w_out, layer, x2, mod_l, ln_g, ln_b, router_w, alpha, seq, tm=256):
    n, d = x2.shape
    e = router_w.shape[2]
    tm = min(tm, seq)
    tps = seq // tm
    kern = functools.partial(_outproj_kernel, alpha=alpha)
    vec = lambda: pl.BlockSpec((None, 1, d), lambda i: (layer, 0, 0))
    row = lambda: pl.BlockSpec((tm, d), lambda i: (i, 0))
    once = pl.Buffered(1)
    return pl.pallas_call(
        kern,
        out_shape=(jax.ShapeDtypeStruct((n, d), F32), jax.ShapeDtypeStruct((n, d), F32),
                   jax.ShapeDtypeStruct((n, e), F32)),
        grid=(n // tm,),
        in_specs=[row(),
                  pl.BlockSpec((None, d, d), lambda i: (layer, 0, 0), pipeline_mode=once),
                  row(),
                  _mod_spec(2, d, tps), vec(), vec(), _mod_spec(3, d, tps), _mod_spec(4, d, tps),
                  pl.BlockSpec((None, d, e), lambda i: (layer, 0, 0), pipeline_mode=once)],
        out_specs=(row(), row(), pl.BlockSpec((tm, e), lambda i: (i, 0))),
        scratch_shapes=[pltpu.VMEM((d, d), BF16)],
        compiler_params=_cparams(1),
        name="out_proj_norm",
    )(merged, w_out, x2, mod_l, ln_g.reshape(-1, 1, d), ln_b.reshape(-1, 1, d), mod_l, mod_l, router_w)


def _route_kernel(lg_ref, bias_ref, idx_ref, w_ref, pos_ref, cnt_ref, carry_ref, *, tm):
    i = pl.program_id(0)

    @pl.when(i == 0)
    def _():
        carry_ref[...] = jnp.zeros_like(carry_ref)

    aff = jax.nn.sigmoid(lg_ref[...].T)
    e = aff.shape[0]
    sub = lax.broadcasted_iota(I32, (e, tm), 0).astype(F32)
    work = aff + bias_ref[...]
    picks = []
    chosen_f = jnp.zeros((e, tm), F32)
    for _ in range(TOP_K):
        best = jnp.max(work, axis=0, keepdims=True)
        pick = jnp.min(jnp.where(work == best, sub, float(e)), axis=0, keepdims=True)
        onehot = sub == pick
        picks.append((pick, onehot))
        chosen_f = jnp.where(onehot, 1.0, chosen_f)
        work = jnp.where(onehot, NEG_INF, work)

    sel_aff = chosen_f * aff
    dense_w = sel_aff / jnp.sum(sel_aff, axis=0, keepdims=True) * ROUTED_SCALE
    r = lax.broadcasted_iota(I32, (tm, tm), 0)
    c = lax.broadcasted_iota(I32, (tm, tm), 1)
    earlier = (r < c).astype(BF16)
    rank = _dot(chosen_f.astype(BF16), earlier) + carry_ref[...]
    carry_ref[...] += jnp.sum(chosen_f, axis=1, keepdims=True)
    cnt_ref[...] = carry_ref[...].astype(I32)

    slot = lax.broadcasted_iota(I32, (TOP_K, tm), 0)
    idx8 = jnp.zeros((TOP_K, tm), F32)
    w8 = jnp.zeros((TOP_K, tm), F32)
    pos8 = jnp.zeros((TOP_K, tm), F32)
    for kk, (pick, onehot) in enumerate(picks):
        idx8 = jnp.where(slot == kk, pick, idx8)
        w8 = jnp.where(slot == kk, jnp.sum(jnp.where(onehot, dense_w, 0.0), axis=0, keepdims=True), w8)
        pos8 = jnp.where(slot == kk, jnp.sum(jnp.where(onehot, rank, 0.0), axis=0, keepdims=True), pos8)
    idx_ref[...] = idx8.astype(I32)
    w_ref[...] = w8
    pos_ref[...] = pos8.astype(I32)


def _route(logits, router_bias, layer, tm=512):
    n, e = logits.shape
    kern = functools.partial(_route_kernel, tm=tm)
    slots = lambda: pl.BlockSpec((TOP_K, tm), lambda i: (0, i))
    return pl.pallas_call(
        kern,
        out_shape=(jax.ShapeDtypeStruct((TOP_K, n), I32), jax.ShapeDtypeStruct((TOP_K, n), F32),
                   jax.ShapeDtypeStruct((TOP_K, n), I32), jax.ShapeDtypeStruct((e, 1), I32)),
        grid=(n // tm,),
        in_specs=[pl.BlockSpec((tm, e), lambda i: (i, 0)),
                  pl.BlockSpec((None, e, 1), lambda i: (layer, 0, 0))],
        out_specs=(slots(), slots(), slots(), pl.BlockSpec((e, 1), lambda i: (0, 0))),
        scratch_shapes=[pltpu.VMEM((e, 1), F32)],
        compiler_params=_cparams(1),
        name="route",
    )(logits, router_bias.reshape(-1, e, 1))


def _dispatch_kernel(cnt_ref, start_ref, nu_ref, u_ref, idx_ref, pos_ref, dest_ref, xs_hbm,
                     slab_ref, dest_smem, zrow_ref, zblk_ref, idx_sem, row_sem, zero_sem,
                     *, tm, tr, n_tiles, n_blocks, n_experts):
    i = pl.program_id(0)
    slab_ref[...] = u_ref[...].reshape(slab_ref.shape)
    idx = idx_ref[...]
    dest = pos_ref[...]
    for e in range(n_experts):
        dest = dest + jnp.where(idx == e, start_ref[e], 0)
    dest_ref[...] = dest
    idx_cp = pltpu.make_async_copy(dest_ref, dest_smem, idx_sem)
    idx_cp.start()
    idx_cp.wait()

    def issue(g, carry):
        base = pl.multiple_of(g * 8, 8)
        for u in range(8):
            for kk in range(TOP_K):
                dst = dest_smem[kk, base + u]
                pltpu.make_async_copy(slab_ref.at[base + u], xs_hbm.at[dst], row_sem).start(priority=kk % 2)
        return carry

    lax.fori_loop(0, tm // 8, issue, 0)
    for kk in range(TOP_K):
        pltpu.make_async_copy(slab_ref, xs_hbm.at[pl.ds(0, tm)], row_sem).wait()

    @pl.when(i == n_tiles - 1)
    def _():
        zrow_ref[...] = jnp.zeros_like(zrow_ref)
        zblk_ref[...] = jnp.zeros_like(zblk_ref)

        def pad_expert(e, carry):
            cnt = cnt_ref[e]
            first = start_ref[e] + cnt
            n_pad = (tr - cnt % tr) % tr

            def start(r, c):
                pltpu.make_async_copy(zrow_ref, xs_hbm.at[first + r], zero_sem).start()
                return c

            def wait(r, c):
                pltpu.make_async_copy(zrow_ref, xs_hbm.at[0], zero_sem).wait()
                return c

            lax.fori_loop(0, n_pad, start, 0)
            lax.fori_loop(0, n_pad, wait, 0)
            return carry

        lax.fori_loop(0, n_experts, pad_expert, 0)

        def tail_start(blk, c):
            row0 = pl.multiple_of(blk * tr, tr)
            pltpu.make_async_copy(zblk_ref, xs_hbm.at[pl.ds(row0, tr)], zero_sem).start()
            return c

        def tail_wait(blk, c):
            pltpu.make_async_copy(zblk_ref, xs_hbm.at[pl.ds(0, tr)], zero_sem).wait()
            return c

        lax.fori_loop(nu_ref[0], n_blocks, tail_start, 0)
        lax.fori_loop(nu_ref[0], n_blocks, tail_wait, 0)


def _dispatch(u2, idx_t, pos_t, counts, start_padded, n_used, tr, n_blocks, tm=512):
    n, d = u2.shape
    n_tiles = n // tm
    slab = (d // LANES, LANES)
    kern = functools.partial(_dispatch_kernel, tm=tm, tr=tr, n_tiles=n_tiles, n_blocks=n_blocks,
                             n_experts=counts.shape[0])
    slots = lambda: pl.BlockSpec((TOP_K, tm), lambda i, *_: (0, i))
    grid_spec = pltpu.PrefetchScalarGridSpec(
        num_scalar_prefetch=3,
        grid=(n_tiles,),
        in_specs=[pl.BlockSpec((tm, d), lambda i, *_: (i, 0)), slots(), slots()],
        out_specs=(slots(), pl.BlockSpec(memory_space=pl.ANY)),
        scratch_shapes=[pltpu.VMEM((tm,) + slab, F32), pltpu.SMEM((TOP_K, tm), I32),
                        pltpu.VMEM(slab, F32), pltpu.VMEM((tr,) + slab, F32),
                        pltpu.SemaphoreType.DMA(()), pltpu.SemaphoreType.DMA(()), pltpu.SemaphoreType.DMA(())])
    dest_t, xs = pl.pallas_call(
        kern,
        out_shape=(jax.ShapeDtypeStruct((TOP_K, n), I32), jax.ShapeDtypeStruct((n_blocks * tr,) + slab, F32)),
        grid_spec=grid_spec,
        compiler_params=_cparams(1),
        name="dispatch",
    )(counts, start_padded, n_used, u2, idx_t, pos_t)
    return xs, dest_t


def _experts_kernel(be_ref, bs_ref, nxt_ref, nu_ref, x_ref, wg_hbm, wu_hbm, wd_hbm, y_ref,
                    wg_buf, wu_buf, wd_buf, wgu_bf, wd_bf, h_buf, state_ref, sems, *, hid, layer, n_blocks):
    i = pl.program_id(0)
    n_used = nu_ref[0]
    blk = jnp.minimum(i, n_blocks - 1)

    def fetch(e, s):
        return (pltpu.make_async_copy(wg_hbm.at[layer, e], wg_buf.at[s], sems.at[s, 0]),
                pltpu.make_async_copy(wu_hbm.at[layer, e], wu_buf.at[s], sems.at[s, 1]),
                pltpu.make_async_copy(wd_hbm.at[layer, e], wd_buf.at[s], sems.at[s, 2]))

    @pl.when(i == 0)
    def _():
        state_ref[0] = 0
        state_ref[1] = 0
        for cp in fetch(be_ref[0], 0):
            cp.start()

    down_slot = state_ref[1]
    e = be_ref[blk]
    changed = jnp.logical_and(i < n_used, jnp.logical_or(i == 0, e != be_ref[jnp.maximum(blk - 1, 0)]))

    @pl.when(changed)
    def _():
        s = state_ref[0]
        for cp in fetch(e, s):
            cp.wait()
        nxt = nxt_ref[e]

        @pl.when(nxt >= 0)
        def _():
            for cp in fetch(nxt, 1 - s):
                cp.start()

        wgu_bf[:, :hid] = wg_buf[s].astype(BF16)
        wgu_bf[:, hid:] = wu_buf[s].astype(BF16)
        wd_bf[1 - down_slot] = wd_buf[s].astype(BF16)
        state_ref[0] = 1 - s
        state_ref[1] = 1 - down_slot

    def gate_up():
        x = x_ref[...].reshape(x_ref.shape[0], -1)
        gu = _dot(x.astype(BF16), wgu_bf[...])
        g = gu[:, :hid]
        h_buf[i % 2] = ((g * jax.nn.sigmoid(g)) * gu[:, hid:]).astype(BF16)

    def down():
        y_ref[...] = _dot(h_buf[(i + 1) % 2], wd_bf[down_slot]).reshape(y_ref.shape)

    @pl.when(jnp.logical_and(i >= 1, i < n_used))
    def _():
        down()
        gate_up()

    @pl.when(i == 0)
    def _():
        gate_up()

    @pl.when(i == n_used)
    def _():
        down()

    @pl.when(i > n_used)
    def _():
        y_ref[...] = jnp.zeros_like(y_ref)


def _experts(xs, block_e, block_src, next_e, n_used, w_gate, w_up, w_down, layer, tr):
    rows = xs.shape[0]
    slab = xs.shape[1:]
    d = slab[0] * slab[1]
    n_blocks = rows // tr
    hid = w_gate.shape[3]
    kern = functools.partial(_experts_kernel, hid=hid, layer=layer, n_blocks=n_blocks)
    hbm = lambda: pl.BlockSpec(memory_space=pl.ANY)
    grid_spec = pltpu.PrefetchScalarGridSpec(
        num_scalar_prefetch=4,
        grid=(n_blocks + 1,),
        in_specs=[pl.BlockSpec((tr,) + slab, lambda i, be, bs, nx, nu: (bs[jnp.minimum(i, n_blocks - 1)], 0, 0)),
                  hbm(), hbm(), hbm()],
        out_specs=pl.BlockSpec((tr,) + slab, lambda i, be, bs, nx, nu: (jnp.maximum(i - 1, 0), 0, 0)),
        scratch_shapes=[pltpu.VMEM((2, d, hid), F32), pltpu.VMEM((2, d, hid), F32), pltpu.VMEM((2, hid, d), F32),
                        pltpu.VMEM((d, 2 * hid), BF16), pltpu.VMEM((2, hid, d), BF16),
                        pltpu.VMEM((2, tr, hid), BF16),
                        pltpu.SMEM((2,), I32), pltpu.SemaphoreType.DMA((2, 3))])
    return pl.pallas_call(
        kern,
        out_shape=jax.ShapeDtypeStruct((rows,) + slab, F32),
        grid_spec=grid_spec,
        compiler_params=_cparams(1),
        name="routed_experts",
    )(block_e, block_src, next_e, n_used, xs, w_gate, w_up, w_down)


def _shared_kernel(u_ref, wg_ref, wu_ref, wd_ref, o_ref, wg_bf, wu_bf, wd_bf):
    @pl.when(pl.program_id(0) == 0)
    def _():
        wg_bf[...] = wg_ref[...].astype(BF16)
        wu_bf[...] = wu_ref[...].astype(BF16)
        wd_bf[...] = wd_ref[...].astype(BF16)

    xb = u_ref[...].astype(BF16)
    g = _dot(xb, wg_bf[...])
    h = (g * jax.nn.sigmoid(g)) * _dot(xb, wu_bf[...])
    o_ref[...] = _dot(h.astype(BF16), wd_bf[...])


def _shared_expert(u2, sh_gate, sh_up, sh_down, layer, tm=512):
    n, d = u2.shape
    hid = sh_gate.shape[2]
    return pl.pallas_call(
        _shared_kernel,
        out_shape=jax.ShapeDtypeStruct((n, d), F32),
        grid=(n // tm,),
        in_specs=[pl.BlockSpec((tm, d), lambda i: (i, 0)),
                  pl.BlockSpec((None, d, hid), lambda i: (layer, 0, 0)),
                  pl.BlockSpec((None, d, hid), lambda i: (layer, 0, 0)),
                  pl.BlockSpec((None, hid, d), lambda i: (layer, 0, 0))],
        out_specs=pl.BlockSpec((tm, d), lambda i: (i, 0)),
        scratch_shapes=[pltpu.VMEM((d, hid), BF16), pltpu.VMEM((d, hid), BF16), pltpu.VMEM((hid, d), BF16)],
        compiler_params=_cparams(1),
        name="shared_expert",
    )(u2, sh_gate, sh_up, sh_down)


def _combine_kernel(dest_ref, wt_ref, y_hbm, shd_ref, x_ref, gate_ref, g_ref, b_ref, sh_ref, sc_ref,
                    xo_ref, uo_ref, idx_smem, w_smem, ybuf, acc_ref, idx_sem, row_sem, *, tm, n_tiles, alpha):
    i = pl.program_id(0)
    slot = i % 2

    def gather(tile, s):
        copies = (pltpu.make_async_copy(dest_ref.at[tile], idx_smem.at[s], idx_sem.at[0]),
                  pltpu.make_async_copy(wt_ref.at[tile], w_smem.at[s], idx_sem.at[1]))
        for cp in copies:
            cp.start()
        for cp in copies:
            cp.wait()

        def issue(g, carry):
            base = pl.multiple_of(g * 8, 8)
            for u in range(8):
                for kk in range(TOP_K):
                    src = idx_smem[s, kk, base + u]
                    pltpu.make_async_copy(y_hbm.at[src], ybuf.at[s, kk, base + u],
                                          row_sem.at[s]).start(priority=kk % 2)
            return carry

        lax.fori_loop(0, tm // 8, issue, 0)

    @pl.when(i == 0)
    def _():
        gather(0, 0)

    @pl.when(i + 1 < n_tiles)
    def _():
        gather(i + 1, 1 - slot)

    for kk in range(TOP_K):
        pltpu.make_async_copy(y_hbm.at[pl.ds(0, tm)], ybuf.at[slot, kk], row_sem.at[slot]).wait()

    def weigh(t, carry):
        acc = w_smem[slot, 0, t] * ybuf[slot, 0, t]
        for kk in range(1, TOP_K):
            acc = acc + w_smem[slot, kk, t] * ybuf[slot, kk, t]
        acc_ref[t] = acc
        return carry

    lax.fori_loop(0, tm, weigh, 0, unroll=4)
    ffn = shd_ref[...] + acc_ref[...].reshape(shd_ref.shape)
    xn = _post_norm(x_ref[...], gate_ref[...], ffn, g_ref[...], b_ref[...], alpha)
    xo_ref[...] = xn
    uo_ref[...] = (xn * (1.0 + sc_ref[...]) + sh_ref[...]).astype(uo_ref.dtype)


def _combine_norm(dest_t, w_t, y_sorted, shared, x2, mod_l, mod_next, ln_g, ln_b, layer, alpha, seq, tm=128):
    n, d = x2.shape
    slab = y_sorted.shape[1:]
    n_tiles = n // tm
    tm = min(tm, seq)
    tps = seq // tm
    kern = functools.partial(_combine_kernel, tm=tm, n_tiles=n_tiles, alpha=alpha)
    row = lambda: pl.BlockSpec((tm, d), lambda i: (i, 0))
    vec = lambda: pl.BlockSpec((None, 1, d), lambda i: (layer, 0, 0))
    per_tile = lambda a: a.reshape(TOP_K, n_tiles, tm).transpose(1, 0, 2)
    table = lambda: pl.BlockSpec((n_tiles, TOP_K, tm), lambda i: (0, 0, 0))
    return pl.pallas_call(
        kern,
        out_shape=(jax.ShapeDtypeStruct((n, d), F32), jax.ShapeDtypeStruct((n, d), BF16)),
        grid=(n_tiles,),
        in_specs=[table(), table(),
                  pl.BlockSpec(memory_space=pl.ANY),
                  row(), row(), _mod_spec(5, d, tps), vec(), vec(),
                  _mod_spec(0, d, tps), _mod_spec(1, d, tps)],
        out_specs=(row(), row()),
        scratch_shapes=[pltpu.SMEM((2, TOP_K, tm), I32), pltpu.SMEM((2, TOP_K, tm), F32),
                        pltpu.VMEM((2, TOP_K, tm) + slab, F32), pltpu.VMEM((tm,) + slab, F32),
                        pltpu.SemaphoreType.DMA((2,)), pltpu.SemaphoreType.DMA((2,))],
        compiler_params=_cparams(1),
        name="combine_norm",
    )(per_tile(dest_t), per_tile(w_t), y_sorted, shared, x2, mod_l,
      ln_g.reshape(-1, 1, d), ln_b.reshape(-1, 1, d), mod_next, mod_next)


def _dispatch_plan(counts, tr, n_blocks):
    padded = (counts + tr - 1) // tr * tr
    end_padded = jnp.cumsum(padded)
    start_padded = (end_padded - padded).astype(I32)
    n_used = (end_padded[-1] // tr).astype(I32)
    block = jnp.arange(n_blocks, dtype=I32)
    block_src = jnp.minimum(block, n_used - 1)
    block_e = jnp.sum((block_src[:, None] * tr >= end_padded[None, :]).astype(I32), axis=1)
    n_e = counts.shape[0]
    ids = jnp.where(counts > 0, jnp.arange(n_e, dtype=I32), n_e)
    later = jnp.concatenate([lax.cummin(ids[::-1])[::-1][1:], jnp.full((1,), n_e, I32)])
    next_e = jnp.where(later < n_e, later, -1).astype(I32)
    return start_padded, block_e.astype(I32), block_src, next_e, n_used.reshape(1)


def kernel(x, c, ada_w, ada_b, w_in, gate_b, pool_w, pool_scale, conv_w, w_branch, w_out, ln1_g, ln1_b,
           router_w, router_bias, exp_w_gate, exp_w_up, exp_w_down, sh_w_gate, sh_w_up, sh_w_down, ln2_g, ln2_b):
    b, seq, d = x.shape
    depth = ada_w.shape[0]
    n = b * seq
    alpha = (2 * depth) ** 0.25
    expert_rows = 256
    n_blocks = -(-(n * TOP_K + N_EXPERTS * (expert_rows - 1)) // expert_rows)

    w = BRANCH_WIDTH
    cb = w // LANES
    sq, sk, sv = 0, cb, 2 * cb
    pool_col = 3 * w
    mq_col = 4 * w
    mq, mk, mv = mq_col // LANES, mq_col // LANES + cb, mq_col // LANES + 2 * cb
    conv_col = 7 * w
    gate_col = 10 * w

    mod = _ada(c, ada_w, ada_b)
    cos_t, sin_t = _rope_tables(seq)
    x2 = x.reshape(n, d)
    u = _modulate(x2, mod[0], seq)
    for l in range(depth):
        proj = _inproj(u, w_in, l, cos_t, sin_t, seq, rot_col=mq_col)
        proj3 = proj.reshape(b, seq, -1)
        y_sb = _sb_attention(proj3, sq, sk, sv).reshape(n, w)
        y_mb = _moba_attention(proj3, mq, mk, mv).reshape(n, w)
        y_pool, y_cv = _local_mixers(proj, pool_w[l], pool_scale[l], conv_w[l], seq,
                                     pool_col // w, conv_col // w, conv_col // w + 1, conv_col // w + 2)
        merged = _merge((y_sb, y_pool, y_mb, y_cv), proj, w_branch, gate_b[l], l, gate_col)
        x2, u2, logits = _outproj_norm(merged, w_out, l, x2, mod[l], ln1_g, ln1_b, router_w, alpha, seq)
        idx_t, w_t, pos_t, counts = _route(logits, router_bias, l)
        counts = counts[:, 0]
        start_padded, block_e, block_src, next_e, n_used = _dispatch_plan(counts, expert_rows, n_blocks)
        xs, dest_t = _dispatch(u2, idx_t, pos_t, counts, start_padded, n_used, expert_rows, n_blocks)
        y_sorted = _experts(xs, block_e, block_src, next_e, n_used, exp_w_gate, exp_w_up, exp_w_down, l,
                            expert_rows)
        shared = _shared_expert(u2, sh_w_gate, sh_w_up, sh_w_down, l)
        x2, u = _combine_norm(dest_t, w_t, y_sorted, shared, x2, mod[l], mod[(l + 1) % depth],
                              ln2_g, ln2_b, l, alpha, seq)
    return x2.reshape(b, seq, d)
```

```python
import functools

import jax
import jax.numpy as jnp
from jax import lax
from jax.experimental import pallas as pl
from jax.experimental.pallas import tpu as pltpu

F32 = jnp.float32
BF16 = jnp.bfloat16
I32 = jnp.int32

HEAD_DIM = 64
N_BRANCH = 4
BRANCH_WIDTH = 512
POOL_WINDOWS = (2, 4, 8, 16)
POOL_GROUP = 128
POOL_HALO = 16
CONV_K = 3
MOBA_BLOCK = 256
MOBA_TOPK = 3
ROPE_THETA = 10000.0
N_EXPERTS = 64
TOP_K = 8
EXPERT_HIDDEN = 384
ROUTED_SCALE = 2.5
LN_EPS = 1e-5
NEG_INF = float("-inf")
SB_DEAD_LOG = -100.0
MASKED = -1e30

LANES = 128
VMEM_LIMIT = 56 * 1024 * 1024


def _cparams(n_axes, vmem=VMEM_LIMIT):
    return pltpu.CompilerParams(dimension_semantics=("arbitrary",) * n_axes, vmem_limit_bytes=vmem)


def _dot(a, b):
    return jnp.dot(a, b, preferred_element_type=F32)


def _dot_nt(a, b):
    return lax.dot_general(a, b, (((1,), (1,)), ((), ())), preferred_element_type=F32)


def _split_bf16(x):
    hi = x.astype(BF16)
    lo = (x - hi.astype(F32)).astype(BF16)
    return hi, lo


def _ada_kernel(c_ref, w_ref, b_ref, o_ref):
    c = c_ref[...]
    cond = c * jax.nn.sigmoid(c)
    o_ref[...] = jnp.dot(cond, w_ref[...], preferred_element_type=F32,
                         precision=lax.Precision.HIGHEST) + b_ref[...]


def _ada(c, ada_w, ada_b):
    depth, d, d6 = ada_w.shape
    b = c.shape[0]
    rows = 8
    tn = 1024
    c8 = jnp.zeros((rows, d), F32).at[:b].set(c)
    out = pl.pallas_call(
        _ada_kernel,
        out_shape=jax.ShapeDtypeStruct((depth, rows, d6), F32),
        grid=(depth, d6 // tn),
        in_specs=[pl.BlockSpec((rows, d), lambda l, n: (0, 0)),
                  pl.BlockSpec((None, d, tn), lambda l, n: (l, 0, n)),
                  pl.BlockSpec((None, 1, tn), lambda l, n: (l, 0, n))],
        out_specs=pl.BlockSpec((None, rows, tn), lambda l, n: (l, 0, n)),
        compiler_params=_cparams(2),
        name="ada_mod",
    )(c8, ada_w, ada_b.reshape(depth, 1, d6))
    return out[:, :b].reshape(depth, b, 6, 1, d)


def _mod_spec(which, d, tiles_per_seq):
    return pl.BlockSpec((None, None, 1, d), lambda i, *_: (i // tiles_per_seq, which, 0, 0))


def _modulate_kernel(x_ref, sh_ref, sc_ref, u_ref):
    u_ref[...] = (x_ref[...] * (1.0 + sc_ref[...]) + sh_ref[...]).astype(u_ref.dtype)


def _modulate(x2, mod_l, seq, tm=512):
    n, d = x2.shape
    tm = min(tm, seq)
    tps = seq // tm
    return pl.pallas_call(
        _modulate_kernel,
        out_shape=jax.ShapeDtypeStruct((n, d), BF16),
        grid=(n // tm,),
        in_specs=[pl.BlockSpec((tm, d), lambda i: (i, 0)), _mod_spec(0, d, tps), _mod_spec(1, d, tps)],
        out_specs=pl.BlockSpec((tm, d), lambda i: (i, 0)),
        compiler_params=_cparams(1),
        name="modulate",
    )(x2, mod_l, mod_l)


def _inproj_kernel(u_ref, w_ref, cos_ref, sin_ref, o_ref, wbf_ref, *, rot_tile, tn):
    n = pl.program_id(0)
    m = pl.program_id(1)

    @pl.when(m == 0)
    def _():
        wbf_ref[...] = w_ref[...].astype(BF16)

    acc = _dot(u_ref[...], wbf_ref[...])

    @pl.when(n == rot_tile)
    def _():
        reps = tn // LANES
        cos = jnp.concatenate([cos_ref[...]] * reps, axis=1)
        sin = jnp.concatenate([sin_ref[...]] * reps, axis=1)
        lane = lax.broadcasted_iota(I32, acc.shape, 1)
        first_half = (lane % HEAD_DIM) < (HEAD_DIM // 2)
        half = HEAD_DIM // 2
        partner = jnp.where(first_half, pltpu.roll(acc, tn - half, 1), pltpu.roll(acc, half, 1))
        o_ref[...] = (acc * cos + partner * sin).astype(o_ref.dtype)

    @pl.when(n != rot_tile)
    def _():
        o_ref[...] = acc.astype(o_ref.dtype)


def _inproj(u, w_in, layer, cos_t, sin_t, seq, rot_col, tm=1024, tn=1024):
    n, d = u.shape
    cols = w_in.shape[2]
    tm = min(tm, seq)
    tps = seq // tm
    assert rot_col % tn == 0
    kern = functools.partial(_inproj_kernel, rot_tile=rot_col // tn, tn=tn)
    return pl.pallas_call(
        kern,
        out_shape=jax.ShapeDtypeStruct((n, cols), BF16),
        grid=(cols // tn, n // tm),
        in_specs=[pl.BlockSpec((tm, d), lambda j, i: (i, 0)),
                  pl.BlockSpec((None, d, tn), lambda j, i: (layer, 0, j)),
                  pl.BlockSpec((tm, LANES), lambda j, i: (i % tps, 0)),
                  pl.BlockSpec((tm, LANES), lambda j, i: (i % tps, 0))],
        out_specs=pl.BlockSpec((tm, tn), lambda j, i: (i, j)),
        scratch_shapes=[pltpu.VMEM((d, tn), BF16)],
        compiler_params=_cparams(2),
        name="in_proj",
    )(u, w_in, cos_t, sin_t)


def _rope_tables(seq):
    half = HEAD_DIM // 2
    inv_freq = ROPE_THETA ** (-jnp.arange(half, dtype=F32) / half)
    ang = jnp.arange(seq, dtype=F32)[:, None] * inv_freq[None, :]
    cos, sin = jnp.cos(ang), jnp.sin(ang)
    cos_h = jnp.concatenate([cos, cos], axis=1)
    sin_h = jnp.concatenate([-sin, sin], axis=1)
    reps = LANES // HEAD_DIM
    return jnp.tile(cos_h, (1, reps)), jnp.tile(sin_h, (1, reps))


def _sb_kernel(q_ref, k_ref, v_ref, o_ref, *, tq, tk):
    qi = pl.program_id(2)
    q0 = qi * tq
    r = lax.broadcasted_iota(I32, (tk, tk), 0)
    c = lax.broadcasted_iota(I32, (tk, tk), 1)
    later = (r > c).astype(BF16)
    heads = LANES // HEAD_DIM
    qs = [q_ref[:, h * HEAD_DIM:(h + 1) * HEAD_DIM] * (HEAD_DIM ** -0.5) for h in range(heads)]

    def block(j0, h, carry, acc, masked):
        k = k_ref[pl.ds(j0, tk), h * HEAD_DIM:(h + 1) * HEAD_DIM]
        v = v_ref[pl.ds(j0, tk), h * HEAD_DIM:(h + 1) * HEAD_DIM]
        z = _dot_nt(qs[h], k)
        softplus = jnp.maximum(z, 0.0) + jnp.log(1.0 + jnp.exp(-jnp.abs(z)))
        log_keep = -softplus
        if masked:
            qpos = q0 + lax.broadcasted_iota(I32, (tq, tk), 0)
            kpos = j0 + lax.broadcasted_iota(I32, (tq, tk), 1)
            past = kpos < qpos
            log_keep = jnp.where(past, log_keep, 0.0)
        hi, lo = _split_bf16(log_keep)
        between = _dot(hi, later) + _dot(lo, later) + carry
        w = jnp.exp((z - softplus) + between)
        if masked:
            w = jnp.where(past, w, 0.0)
        acc = acc + _dot(w.astype(BF16), v)
        carry = carry + jnp.sum(log_keep, axis=1, keepdims=True)
        return carry, acc

    state = [(jnp.zeros((tq, 1), F32), jnp.zeros((tq, HEAD_DIM), F32)) for _ in range(heads)]
    n_diag = tq // tk
    for d in range(n_diag - 1, -1, -1):
        j0 = pl.multiple_of(q0 + d * tk, tk)
        state = [block(j0, h, *state[h], True) for h in range(heads)]

    def live(pairs):
        return functools.reduce(jnp.maximum, [jnp.max(c) for c, _ in pairs])

    def cond(st):
        return jnp.logical_and(st[0] < qi, st[1] > SB_DEAD_LOG)

    def body(st):
        jj = st[0]
        new = [(st[2 + 2 * h], st[3 + 2 * h]) for h in range(heads)]
        for part in range(n_diag):
            j0 = pl.multiple_of(q0 - (jj * n_diag + part + 1) * tk, tk)
            new = [block(j0, h, *new[h], False) for h in range(heads)]
        return (jj + 1, live(new)) + tuple(x for pair in new for x in pair)

    final = lax.while_loop(cond, body, (jnp.int32(0), live(state)) + tuple(x for pair in state for x in pair))
    o_ref[...] = jnp.concatenate([final[3 + 2 * h] for h in range(heads)], axis=1).astype(o_ref.dtype)


def _sb_attention(proj3, q_blk, k_blk, v_blk, tq=256, tk=128):
    b, seq, _ = proj3.shape
    pairs = BRANCH_WIDTH // LANES
    kern = functools.partial(_sb_kernel, tq=tq, tk=tk)
    return pl.pallas_call(
        kern,
        out_shape=jax.ShapeDtypeStruct((b, seq, BRANCH_WIDTH), BF16),
        grid=(b, pairs, seq // tq),
        in_specs=[pl.BlockSpec((None, tq, LANES), lambda bi, p, qi: (bi, qi, q_blk + p)),
                  pl.BlockSpec((None, seq, LANES), lambda bi, p, qi: (bi, 0, k_blk + p)),
                  pl.BlockSpec((None, seq, LANES), lambda bi, p, qi: (bi, 0, v_blk + p))],
        out_specs=pl.BlockSpec((None, tq, LANES), lambda bi, p, qi: (bi, qi, p)),
        compiler_params=_cparams(3),
        name="sb_attention",
    )(proj3, proj3, proj3)


def _moba_kernel(q_ref, k_ref, v_ref, o_ref, kmean_ref, kt_ref, vaug_ref, s_ref, *, n_kb):
    own = pl.program_id(2)
    tq = MOBA_BLOCK
    heads = LANES // HEAD_DIM
    last = n_kb - 1

    @pl.when(own == 0)
    def _():
        s_ref[...] = jnp.zeros_like(s_ref)
        ones = jnp.ones((MOBA_BLOCK, HEAD_DIM), BF16)
        for jb in range(n_kb):
            rows = slice(jb * MOBA_BLOCK, (jb + 1) * MOBA_BLOCK)
            kb = k_ref[rows, :]
            kmean_ref[jb:jb + 1, :] = jnp.mean(kb.astype(F32), axis=0, keepdims=True)
            kt_ref[jb] = kb.T
            vb = v_ref[rows, :]
            for h in range(heads):
                vaug_ref[h, rows, :] = jnp.concatenate([vb[:, h * HEAD_DIM:(h + 1) * HEAD_DIM], ones], axis=1)

    blk = lax.broadcasted_iota(I32, (n_kb, tq), 0)
    row = lax.broadcasted_iota(I32, (tq, MOBA_BLOCK), 0)
    col = lax.broadcasted_iota(I32, (tq, MOBA_BLOCK), 1)
    causal = col <= row
    block_id = lax.broadcasted_iota(I32, (n_kb, MOBA_BLOCK), 0)

    qs, sel_bias, run_max = [], [], []
    for h in range(heads):
        hs = slice(h * HEAD_DIM, (h + 1) * HEAD_DIM)
        q = q_ref[:, hs] * (HEAD_DIM ** -0.5)
        qs.append(q)
        km_hi, km_lo = _split_bf16(kmean_ref[:, hs])
        gate = _dot_nt(km_hi, q) + _dot_nt(km_lo, q)
        beaten = jnp.zeros((n_kb, tq), I32)
        for jp in range(n_kb):
            g = gate[jp:jp + 1, :]
            beats = (g > gate) | ((g == gate) & (jp < blk))
            beaten = beaten + jnp.where(beats, 1, 0) * (jp < own).astype(I32)
        chosen = (beaten < MOBA_TOPK) & (blk < own)
        sel_bias.append(jnp.where(chosen, 0.0, MASKED).T.astype(BF16))
        run_max.append(jnp.full((tq, MOBA_BLOCK), NEG_INF, F32))

    def score_pass(t, running):
        out = list(running)
        for half in range(2):
            j = 2 * t + half
            valid = j < own
            jc = jnp.minimum(j, last)
            pick = (block_id == jc).astype(BF16)
            for h in range(heads):
                hs = slice(h * HEAD_DIM, (h + 1) * HEAD_DIM)
                s = _dot(qs[h], kt_ref[jc, hs, :]) + _dot(sel_bias[h], pick)
                s_ref[h, jnp.where(valid, j, n_kb)] = s
                out[h] = jnp.where(valid, jnp.maximum(out[h], s), out[h])
        return tuple(out)

    run_max = lax.fori_loop(0, (own + 1) // 2, score_pass, tuple(run_max))
    row_max = []
    for h in range(heads):
        hs = slice(h * HEAD_DIM, (h + 1) * HEAD_DIM)
        s = jnp.where(causal, _dot(qs[h], kt_ref[own, hs, :]), NEG_INF)
        s_ref[h, own] = s
        row_max.append(jnp.max(jnp.maximum(run_max[h], s), axis=1, keepdims=True))

    def softmax_pass(t, accs):
        out = list(accs)
        for half in range(2):
            j = 2 * t + half
            valid = j <= own
            jc = jnp.minimum(j, last)
            j0 = pl.multiple_of(jc * MOBA_BLOCK, MOBA_BLOCK)
            for h in range(heads):
                p = jnp.where(valid, jnp.exp(s_ref[h, jc] - row_max[h]), 0.0)
                out[h] = out[h] + _dot(p.astype(BF16), vaug_ref[h, pl.ds(j0, MOBA_BLOCK), :])
        return tuple(out)

    accs = lax.fori_loop(0, (own + 2) // 2, softmax_pass, tuple(jnp.zeros((tq, LANES), F32) for _ in range(heads)))
    o_ref[...] = jnp.concatenate([a[:, :HEAD_DIM] / a[:, HEAD_DIM:] for a in accs], axis=1).astype(o_ref.dtype)


def _moba_attention(proj3, q_blk, k_blk, v_blk):
    b, seq, _ = proj3.shape
    assert seq % MOBA_BLOCK == 0
    n_kb = seq // MOBA_BLOCK
    pairs = BRANCH_WIDTH // LANES
    kern = functools.partial(_moba_kernel, n_kb=n_kb)
    return pl.pallas_call(
        kern,
        out_shape=jax.ShapeDtypeStruct((b, seq, BRANCH_WIDTH), BF16),
        grid=(b, pairs, n_kb),
        in_specs=[pl.BlockSpec((None, MOBA_BLOCK, LANES), lambda bi, p, qi: (bi, qi, q_blk + p)),
                  pl.BlockSpec((None, seq, LANES), lambda bi, p, qi: (bi, 0, k_blk + p)),
                  pl.BlockSpec((None, seq, LANES), lambda bi, p, qi: (bi, 0, v_blk + p))],
        out_specs=pl.BlockSpec((None, MOBA_BLOCK, LANES), lambda bi, p, qi: (bi, qi, p)),
        scratch_shapes=[pltpu.VMEM((n_kb, LANES), F32),
                        pltpu.VMEM((n_kb, LANES, MOBA_BLOCK), BF16),
                        pltpu.VMEM((LANES // HEAD_DIM, seq, LANES), BF16),
                        pltpu.VMEM((LANES // HEAD_DIM, n_kb + 1, MOBA_BLOCK, MOBA_BLOCK), F32)],
        compiler_params=_cparams(3),
        name="moba_attention",
    )(proj3, proj3, proj3)


def _local_kernel(p_ref, ph_ref, ch_ref, chh_ref, cb_ref, cc_ref, cch_ref, pw_ref, ps_ref, cw_ref,
                  yp_ref, yc_ref, *, tm, tiles_per_seq):
    i = pl.program_id(0)
    seq_tile = i % tiles_per_seq
    has_left = seq_tile > 0

    def with_halo(halo_ref, body_ref):
        halo = jnp.where(has_left, halo_ref[...].astype(F32), 0.0)
        return jnp.concatenate([halo, body_ref[...].astype(F32)], axis=0)

    x = with_halo(ph_ref, p_ref)
    pos = seq_tile * tm + lax.broadcasted_iota(I32, (tm, 1), 0)
    for g, win in enumerate(POOL_WINDOWS):
        gs = slice(g * POOL_GROUP, (g + 1) * POOL_GROUP)
        xg = x[:, gs]
        s, k = xg, 1
        while k < win:
            s = s[k:] + s[:-k]
            k *= 2
        start = POOL_HALO + 1 - win
        window_sum = s[start:start + tm]
        count = jnp.minimum(pos + 1, win).astype(F32)
        mixed = window_sum / count - xg[POOL_HALO:]
        y = _dot(mixed.astype(BF16), pw_ref[g].astype(BF16)) * ps_ref[:, gs]
        yp_ref[:, gs] = y.astype(yp_ref.dtype)

    z = with_halo(chh_ref, ch_ref) * with_halo(cch_ref, cc_ref)
    y = jnp.zeros((tm, z.shape[1]), F32)
    for k in range(CONV_K):
        off = POOL_HALO - (CONV_K - 1) + k
        y = y + cw_ref[k:k + 1, :] * z[off:off + tm]
    yc_ref[...] = (cb_ref[...].astype(F32) * y).astype(yc_ref.dtype)


def _local_mixers(proj, pool_w, pool_scale, conv_w, seq, pool_blk, ch_blk, cb_blk, cc_blk, tm=512):
    n = proj.shape[0]
    w = BRANCH_WIDTH
    tm = min(tm, seq)
    tps = seq // tm
    hpt = tm // POOL_HALO

    def body(blk):
        return pl.BlockSpec((tm, w), lambda i: (i, blk))

    def halo(blk):
        return pl.BlockSpec((POOL_HALO, w), lambda i: (jnp.maximum(i * hpt - 1, 0), blk))

    kern = functools.partial(_local_kernel, tm=tm, tiles_per_seq=tps)
    return pl.pallas_call(
        kern,
        out_shape=(jax.ShapeDtypeStruct((n, w), BF16), jax.ShapeDtypeStruct((n, w), BF16)),
        grid=(n // tm,),
        in_specs=[body(pool_blk), halo(pool_blk), body(ch_blk), halo(ch_blk), body(cb_blk),
                  body(cc_blk), halo(cc_blk),
                  pl.BlockSpec(pool_w.shape, lambda i: (0, 0, 0)),
                  pl.BlockSpec((1, w), lambda i: (0, 0)),
                  pl.BlockSpec((CONV_K, w), lambda i: (0, 0))],
        out_specs=(pl.BlockSpec((tm, w), lambda i: (i, 0)), pl.BlockSpec((tm, w), lambda i: (i, 0))),
        compiler_params=_cparams(1),
        name="pool_conv",
    )(proj, proj, proj, proj, proj, proj, proj, pool_w, pool_scale.reshape(1, w), conv_w)


def _merge_kernel(b0, b1, b2, b3, g0, g1, g2, g3, wb_ref, gb_ref, o_ref, wbf_ref):
    @pl.when(pl.program_id(1) == 0)
    def _():
        wbf_ref[...] = wb_ref[...].astype(BF16)

    merged = None
    for nb, (br, gr) in enumerate(((b0, g0), (b1, g1), (b2, g2), (b3, g3))):
        lifted = _dot(br[...], wbf_ref[nb])
        gate = jax.nn.sigmoid(gr[...].astype(F32) + gb_ref[nb])
        term = gate * lifted
        merged = term if merged is None else merged + term
    o_ref[...] = merged.astype(o_ref.dtype)


def _merge(branches, proj, w_branch, gate_b, layer, gate_col, tm=1024, tn=512):
    n = proj.shape[0]
    d = w_branch.shape[3]
    assert gate_col % tn == 0 and d % tn == 0
    g_specs = [pl.BlockSpec((tm, tn), functools.partial(lambda j, i, nb: (i, (gate_col + nb * d) // tn + j), nb=nb))
               for nb in range(N_BRANCH)]
    b_specs = [pl.BlockSpec((tm, BRANCH_WIDTH), lambda j, i: (i, 0))] * N_BRANCH
    return pl.pallas_call(
        _merge_kernel,
        out_shape=jax.ShapeDtypeStruct((n, d), BF16),
        grid=(d // tn, n // tm),
        in_specs=b_specs + g_specs + [
            pl.BlockSpec((None, N_BRANCH, BRANCH_WIDTH, tn), lambda j, i: (layer, 0, 0, j)),
            pl.BlockSpec((N_BRANCH, 1, tn), lambda j, i: (0, 0, j))],
        out_specs=pl.BlockSpec((tm, tn), lambda j, i: (i, j)),
        scratch_shapes=[pltpu.VMEM((N_BRANCH, BRANCH_WIDTH, tn), BF16)],
        compiler_params=_cparams(2),
        name="gated_merge",
    )(*branches, proj, proj, proj, proj, w_branch, gate_b.reshape(N_BRANCH, 1, d))


def _post_norm(x, gate, y, g, b, alpha):
    h = alpha * x + (1.0 + gate) * y
    mu = jnp.mean(h, axis=1, keepdims=True)
    hc = h - mu
    var = jnp.mean(hc * hc, axis=1, keepdims=True)
    return hc * lax.rsqrt(var + LN_EPS) * g + b


def _outproj_kernel(mg_ref, w_ref, x_ref, gate_ref, g_ref, b_ref, sh_ref, sc_ref, rw_ref,
                    xo_ref, uo_ref, lg_ref, w_bf, *, alpha):
    @pl.when(pl.program_id(0) == 0)
    def _():
        w_bf[...] = w_ref[...].astype(BF16)

    y = _dot(mg_ref[...], w_bf[...])
    xn = _post_norm(x_ref[...], gate_ref[...], y, g_ref[...], b_ref[...], alpha)
    xo_ref[...] = xn
    u = xn * (1.0 + sc_ref[...]) + sh_ref[...]
    uo_ref[...] = u
    u_hi, u_lo = _split_bf16(u)
    w_hi, w_lo = _split_bf16(rw_ref[...])
    lg_ref[...] = _dot(u_hi, w_hi) + _dot(u_lo, w_hi) + _dot(u_hi, w_lo)


def _outproj_norm(merged, w_out, layer, x2, mod_l, ln_g, ln_b, router_w, alpha, seq, tm=256):
    n, d = x2.shape
    e = router_w.shape[2]
    tm = min(tm, seq)
    tps = seq // tm
    kern = functools.partial(_outproj_kernel, alpha=alpha)
    vec = lambda: pl.BlockSpec((None, 1, d), lambda i: (layer, 0, 0))
    row = lambda: pl.BlockSpec((tm, d), lambda i: (i, 0))
    once = pl.Buffered(1)
    return pl.pallas_call(
        kern,
        out_shape=(jax.ShapeDtypeStruct((n, d), F32), jax.ShapeDtypeStruct((n, d), F32),
                   jax.ShapeDtypeStruct((n, e), F32)),
        grid=(n // tm,),
        in_specs=[row(),
                  pl.BlockSpec((None, d, d), lambda i: (layer, 0, 0), pipeline_mode=once),
                  row(),
                  _mod_spec(2, d, tps), vec(), vec(), _mod_spec(3, d, tps), _mod_spec(4, d, tps),
                  pl.BlockSpec((None, d, e), lambda i: (layer, 0, 0), pipeline_mode=once)],
        out_specs=(row(), row(), pl.BlockSpec((tm, e), lambda i: (i, 0))),
        scratch_shapes=[pltpu.VMEM((d, d), BF16)],
        compiler_params=_cparams(1),
        name="out_proj_norm",
    )(merged, w_out, x2, mod_l, ln_g.reshape(-1, 1, d), ln_b.reshape(-1, 1, d), mod_l, mod_l, router_w)


def _route_kernel(lg_ref, bias_ref, idx_ref, w_ref, pos_ref, cnt_ref, carry_ref, *, tm):
    i = pl.program_id(0)

    @pl.when(i == 0)
    def _():
        carry_ref[...] = jnp.zeros_like(carry_ref)

    aff = jax.nn.sigmoid(lg_ref[...].T)
    e = aff.shape[0]
    sub = lax.broadcasted_iota(I32, (e, tm), 0).astype(F32)
    work = aff + bias_ref[...]
    picks = []
    chosen_f = jnp.zeros((e, tm), F32)
    for _ in range(TOP_K):
        best = jnp.max(work, axis=0, keepdims=True)
        pick = jnp.min(jnp.where(work == best, sub, float(e)), axis=0, keepdims=True)
        onehot = sub == pick
        picks.append((pick, onehot))
        chosen_f = jnp.where(onehot, 1.0, chosen_f)
        work = jnp.where(onehot, NEG_INF, work)

    sel_aff = chosen_f * aff
    dense_w = sel_aff / jnp.sum(sel_aff, axis=0, keepdims=True) * ROUTED_SCALE
    r = lax.broadcasted_iota(I32, (tm, tm), 0)
    c = lax.broadcasted_iota(I32, (tm, tm), 1)
    earlier = (r < c).astype(BF16)
    rank = _dot(chosen_f.astype(BF16), earlier) + carry_ref[...]
    carry_ref[...] += jnp.sum(chosen_f, axis=1, keepdims=True)
    cnt_ref[...] = carry_ref[...].astype(I32)

    slot = lax.broadcasted_iota(I32, (TOP_K, tm), 0)
    idx8 = jnp.zeros((TOP_K, tm), F32)
    w8 = jnp.zeros((TOP_K, tm), F32)
    pos8 = jnp.zeros((TOP_K, tm), F32)
    for kk, (pick, onehot) in enumerate(picks):
        idx8 = jnp.where(slot == kk, pick, idx8)
        w8 = jnp.where(slot == kk, jnp.sum(jnp.where(onehot, dense_w, 0.0), axis=0, keepdims=True), w8)
        pos8 = jnp.where(slot == kk, jnp.sum(jnp.where(onehot, rank, 0.0), axis=0, keepdims=True), pos8)
    idx_ref[...] = idx8.astype(I32)
    w_ref[...] = w8
    pos_ref[...] = pos8.astype(I32)


def _route(logits, router_bias, layer, tm=512):
    n, e = logits.shape
    kern = functools.partial(_route_kernel, tm=tm)
    slots = lambda: pl.BlockSpec((TOP_K, tm), lambda i: (0, i))
    return pl.pallas_call(
        kern,
        out_shape=(jax.ShapeDtypeStruct((TOP_K, n), I32), jax.ShapeDtypeStruct((TOP_K, n), F32),
                   jax.ShapeDtypeStruct((TOP_K, n), I32), jax.ShapeDtypeStruct((e, 1), I32)),
        grid=(n // tm,),
        in_specs=[pl.BlockSpec((tm, e), lambda i: (i, 0)),
                  pl.BlockSpec((None, e, 1), lambda i: (layer, 0, 0))],
        out_specs=(slots(), slots(), slots(), pl.BlockSpec((e, 1), lambda i: (0, 0))),
        scratch_shapes=[pltpu.VMEM((e, 1), F32)],
        compiler_params=_cparams(1),
        name="route",
    )(logits, router_bias.reshape(-1, e, 1))


def _dispatch_kernel(cnt_ref, start_ref, nu_ref, u_ref, idx_ref, pos_ref, dest_ref, xs_hbm,
                     slab_ref, dest_smem, zrow_ref, zblk_ref, idx_sem, row_sem, zero_sem,
                     *, tm, tr, n_tiles, n_blocks, n_experts):
    i = pl.program_id(0)
    slab_ref[...] = u_ref[...].astype(slab_ref.dtype).reshape(slab_ref.shape)
    idx = idx_ref[...]
    dest = pos_ref[...]
    for e in range(n_experts):
        dest = dest + jnp.where(idx == e, start_ref[e], 0)
    dest_ref[...] = dest
    idx_cp = pltpu.make_async_copy(dest_ref, dest_smem, idx_sem)
    idx_cp.start()
    idx_cp.wait()

    def issue(g, carry):
        base = pl.multiple_of(g * 8, 8)
        for u in range(8):
            for kk in range(TOP_K):
                dst = dest_smem[kk, base + u]
                pltpu.make_async_copy(slab_ref.at[base + u], xs_hbm.at[dst], row_sem).start(priority=kk % 2)
        return carry

    lax.fori_loop(0, tm // 8, issue, 0)
    for kk in range(TOP_K):
        pltpu.make_async_copy(slab_ref, xs_hbm.at[pl.ds(0, tm)], row_sem).wait()

    @pl.when(i == n_tiles - 1)
    def _():
        zrow_ref[...] = jnp.zeros_like(zrow_ref)
        zblk_ref[...] = jnp.zeros_like(zblk_ref)

        def pad_expert(e, carry):
            cnt = cnt_ref[e]
            first = start_ref[e] + cnt
            n_pad = (tr - cnt % tr) % tr

            def start(r, c):
                pltpu.make_async_copy(zrow_ref, xs_hbm.at[first + r], zero_sem).start()
                return c

            def wait(r, c):
                pltpu.make_async_copy(zrow_ref, xs_hbm.at[0], zero_sem).wait()
                return c

            lax.fori_loop(0, n_pad, start, 0)
            lax.fori_loop(0, n_pad, wait, 0)
            return carry

        lax.fori_loop(0, n_experts, pad_expert, 0)

        def tail_start(blk, c):
            row0 = pl.multiple_of(blk * tr, tr)
            pltpu.make_async_copy(zblk_ref, xs_hbm.at[pl.ds(row0, tr)], zero_sem).start()
            return c

        def tail_wait(blk, c):
            pltpu.make_async_copy(zblk_ref, xs_hbm.at[pl.ds(0, tr)], zero_sem).wait()
            return c

        lax.fori_loop(nu_ref[0], n_blocks, tail_start, 0)
        lax.fori_loop(nu_ref[0], n_blocks, tail_wait, 0)


def _dispatch(u2, idx_t, pos_t, counts, start_padded, n_used, tr, n_blocks, tm=512):
    n, d = u2.shape
    n_tiles = n // tm
    slab = (d // LANES, LANES)
    kern = functools.partial(_dispatch_kernel, tm=tm, tr=tr, n_tiles=n_tiles, n_blocks=n_blocks,
                             n_experts=counts.shape[0])
    slots = lambda: pl.BlockSpec((TOP_K, tm), lambda i, *_: (0, i))
    grid_spec = pltpu.PrefetchScalarGridSpec(
        num_scalar_prefetch=3,
        grid=(n_tiles,),
        in_specs=[pl.BlockSpec((tm, d), lambda i, *_: (i, 0)), slots(), slots()],
        out_specs=(slots(), pl.BlockSpec(memory_space=pl.ANY)),
        scratch_shapes=[pltpu.VMEM((tm,) + slab, BF16), pltpu.SMEM((TOP_K, tm), I32),
                        pltpu.VMEM(slab, BF16), pltpu.VMEM((tr,) + slab, BF16),
                        pltpu.SemaphoreType.DMA(()), pltpu.SemaphoreType.DMA(()), pltpu.SemaphoreType.DMA(())])
    dest_t, xs = pl.pallas_call(
        kern,
        out_shape=(jax.ShapeDtypeStruct((TOP_K, n), I32), jax.ShapeDtypeStruct((n_blocks * tr,) + slab, BF16)),
        grid_spec=grid_spec,
        compiler_params=_cparams(1),
        name="dispatch",
    )(counts, start_padded, n_used, u2, idx_t, pos_t)
    return xs, dest_t


def _experts_kernel(be_ref, bs_ref, nxt_ref, nu_ref, x_ref, wg_hbm, wu_hbm, wd_hbm, y_ref,
                    wg_buf, wu_buf, wd_buf, wgu_bf, wd_bf, h_buf, state_ref, sems, *, hid, layer, n_blocks):
    i = pl.program_id(0)
    n_used = nu_ref[0]
    blk = jnp.minimum(i, n_blocks - 1)

    def fetch(e, s):
        return (pltpu.make_async_copy(wg_hbm.at[layer, e], wg_buf.at[s], sems.at[s, 0]),
                pltpu.make_async_copy(wu_hbm.at[layer, e], wu_buf.at[s], sems.at[s, 1]),
                pltpu.make_async_copy(wd_hbm.at[layer, e], wd_buf.at[s], sems.at[s, 2]))

    @pl.when(i == 0)
    def _():
        state_ref[0] = 0
        state_ref[1] = 0
        for cp in fetch(be_ref[0], 0):
            cp.start()

    down_slot = state_ref[1]
    e = be_ref[blk]
    changed = jnp.logical_and(i < n_used, jnp.logical_or(i == 0, e != be_ref[jnp.maximum(blk - 1, 0)]))

    @pl.when(changed)
    def _():
        s = state_ref[0]
        for cp in fetch(e, s):
            cp.wait()
        nxt = nxt_ref[e]

        @pl.when(nxt >= 0)
        def _():
            for cp in fetch(nxt, 1 - s):
                cp.start()

        wgu_bf[:, :hid] = wg_buf[s].astype(BF16)
        wgu_bf[:, hid:] = wu_buf[s].astype(BF16)
        wd_bf[1 - down_slot] = wd_buf[s].astype(BF16)
        state_ref[0] = 1 - s
        state_ref[1] = 1 - down_slot

    def gate_up():
        x = x_ref[...].reshape(x_ref.shape[0], -1)
        gu = _dot(x, wgu_bf[...])
        g = gu[:, :hid]
        h_buf[i % 2] = ((g * jax.nn.sigmoid(g)) * gu[:, hid:]).astype(BF16)

    def down():
        y_ref[...] = _dot(h_buf[(i + 1) % 2], wd_bf[down_slot]).astype(y_ref.dtype).reshape(y_ref.shape)

    @pl.when(jnp.logical_and(i >= 1, i < n_used))
    def _():
        down()
        gate_up()

    @pl.when(i == 0)
    def _():
        gate_up()

    @pl.when(i == n_used)
    def _():
        down()

    @pl.when(i > n_used)
    def _():
        y_ref[...] = jnp.zeros_like(y_ref)


def _experts(xs, block_e, block_src, next_e, n_used, w_gate, w_up, w_down, layer, tr):
    rows = xs.shape[0]
    slab = xs.shape[1:]
    d = slab[0] * slab[1]
    n_blocks = rows // tr
    hid = w_gate.shape[3]
    kern = functools.partial(_experts_kernel, hid=hid, layer=layer, n_blocks=n_blocks)
    hbm = lambda: pl.BlockSpec(memory_space=pl.ANY)
    grid_spec = pltpu.PrefetchScalarGridSpec(
        num_scalar_prefetch=4,
        grid=(n_blocks + 1,),
        in_specs=[pl.BlockSpec((tr,) + slab, lambda i, be, bs, nx, nu: (bs[jnp.minimum(i, n_blocks - 1)], 0, 0)),
                  hbm(), hbm(), hbm()],
        out_specs=pl.BlockSpec((tr,) + slab, lambda i, be, bs, nx, nu: (jnp.maximum(i - 1, 0), 0, 0)),
        scratch_shapes=[pltpu.VMEM((2, d, hid), F32), pltpu.VMEM((2, d, hid), F32), pltpu.VMEM((2, hid, d), F32),
                        pltpu.VMEM((d, 2 * hid), BF16), pltpu.VMEM((2, hid, d), BF16),
                        pltpu.VMEM((2, tr, hid), BF16),
                        pltpu.SMEM((2,), I32), pltpu.SemaphoreType.DMA((2, 3))])
    return pl.pallas_call(
        kern,
        out_shape=jax.ShapeDtypeStruct((rows,) + slab, xs.dtype),
        grid_spec=grid_spec,
        compiler_params=_cparams(1),
        name="routed_experts",
    )(block_e, block_src, next_e, n_used, xs, w_gate, w_up, w_down)


def _shared_kernel(u_ref, wg_ref, wu_ref, wd_ref, o_ref, wg_bf, wu_bf, wd_bf):
    @pl.when(pl.program_id(0) == 0)
    def _():
        wg_bf[...] = wg_ref[...].astype(BF16)
        wu_bf[...] = wu_ref[...].astype(BF16)
        wd_bf[...] = wd_ref[...].astype(BF16)

    xb = u_ref[...].astype(BF16)
    g = _dot(xb, wg_bf[...])
    h = (g * jax.nn.sigmoid(g)) * _dot(xb, wu_bf[...])
    o_ref[...] = _dot(h.astype(BF16), wd_bf[...])


def _shared_expert(u2, sh_gate, sh_up, sh_down, layer, tm=512):
    n, d = u2.shape
    hid = sh_gate.shape[2]
    return pl.pallas_call(
        _shared_kernel,
        out_shape=jax.ShapeDtypeStruct((n, d), F32),
        grid=(n // tm,),
        in_specs=[pl.BlockSpec((tm, d), lambda i: (i, 0)),
                  pl.BlockSpec((None, d, hid), lambda i: (layer, 0, 0)),
                  pl.BlockSpec((None, d, hid), lambda i: (layer, 0, 0)),
                  pl.BlockSpec((None, hid, d), lambda i: (layer, 0, 0))],
        out_specs=pl.BlockSpec((tm, d), lambda i: (i, 0)),
        scratch_shapes=[pltpu.VMEM((d, hid), BF16), pltpu.VMEM((d, hid), BF16), pltpu.VMEM((hid, d), BF16)],
        compiler_params=_cparams(1),
        name="shared_expert",
    )(u2, sh_gate, sh_up, sh_down)


def _combine_kernel(dest_ref, wt_ref, y_hbm, shd_ref, x_ref, gate_ref, g_ref, b_ref, sh_ref, sc_ref,
                    xo_ref, uo_ref, idx_smem, w_smem, ybuf, acc_ref, idx_sem, row_sem, *, tm, n_tiles, alpha):
    i = pl.program_id(0)
    slot = i % 2

    def gather(tile, s):
        copies = (pltpu.make_async_copy(dest_ref.at[tile], idx_smem.at[s], idx_sem.at[0]),
                  pltpu.make_async_copy(wt_ref.at[tile], w_smem.at[s], idx_sem.at[1]))
        for cp in copies:
            cp.start()
        for cp in copies:
            cp.wait()

        def issue(g, carry):
            base = pl.multiple_of(g * 8, 8)
            for u in range(8):
                for kk in range(TOP_K):
                    src = idx_smem[s, kk, base + u]
                    pltpu.make_async_copy(y_hbm.at[src], ybuf.at[s, kk, base + u],
                                          row_sem.at[s]).start(priority=kk % 2)
            return carry

        lax.fori_loop(0, tm // 8, issue, 0)

    @pl.when(i == 0)
    def _():
        gather(0, 0)

    @pl.when(i + 1 < n_tiles)
    def _():
        gather(i + 1, 1 - slot)

    for kk in range(TOP_K):
        pltpu.make_async_copy(y_hbm.at[pl.ds(0, tm)], ybuf.at[slot, kk], row_sem.at[slot]).wait()

    def weigh(t, carry):
        acc = w_smem[slot, 0, t] * ybuf[slot, 0, t].astype(F32)
        for kk in range(1, TOP_K):
            acc = acc + w_smem[slot, kk, t] * ybuf[slot, kk, t].astype(F32)
        acc_ref[t] = acc
        return carry

    lax.fori_loop(0, tm, weigh, 0, unroll=4)
    ffn = shd_ref[...] + acc_ref[...].reshape(shd_ref.shape)
    xn = _post_norm(x_ref[...], gate_ref[...], ffn, g_ref[...], b_ref[...], alpha)
    xo_ref[...] = xn
    uo_ref[...] = (xn * (1.0 + sc_ref[...]) + sh_ref[...]).astype(uo_ref.dtype)


def _combine_norm(dest_t, w_t, y_sorted, shared, x2, mod_l, mod_next, ln_g, ln_b, layer, alpha, seq, tm=128):
    n, d = x2.shape
    slab = y_sorted.shape[1:]
    n_tiles = n // tm
    tm = min(tm, seq)
    tps = seq // tm
    kern = functools.partial(_combine_kernel, tm=tm, n_tiles=n_tiles, alpha=alpha)
    row = lambda: pl.BlockSpec((tm, d), lambda i: (i, 0))
    vec = lambda: pl.BlockSpec((None, 1, d), lambda i: (layer, 0, 0))
    per_tile = lambda a: a.reshape(TOP_K, n_tiles, tm).transpose(1, 0, 2)
    table = lambda: pl.BlockSpec((n_tiles, TOP_K, tm), lambda i: (0, 0, 0))
    return pl.pallas_call(
        kern,
        out_shape=(jax.ShapeDtypeStruct((n, d), F32), jax.ShapeDtypeStruct((n, d), BF16)),
        grid=(n_tiles,),
        in_specs=[table(), table(),
                  pl.BlockSpec(memory_space=pl.ANY),
                  row(), row(), _mod_spec(5, d, tps), vec(), vec(),
                  _mod_spec(0, d, tps), _mod_spec(1, d, tps)],
        out_specs=(row(), row()),
        scratch_shapes=[pltpu.SMEM((2, TOP_K, tm), I32), pltpu.SMEM((2, TOP_K, tm), F32),
                        pltpu.VMEM((2, TOP_K, tm) + slab, y_sorted.dtype), pltpu.VMEM((tm,) + slab, F32),
                        pltpu.SemaphoreType.DMA((2,)), pltpu.SemaphoreType.DMA((2,))],
        compiler_params=_cparams(1),
        name="combine_norm",
    )(per_tile(dest_t), per_tile(w_t), y_sorted, shared, x2, mod_l,
      ln_g.reshape(-1, 1, d), ln_b.reshape(-1, 1, d), mod_next, mod_next)


def _dispatch_plan(counts, tr, n_blocks):
    padded = (counts + tr - 1) // tr * tr
    end_padded = jnp.cumsum(padded)
    start_padded = (end_padded - padded).astype(I32)
    n_used = (end_padded[-1] // tr).astype(I32)
    block = jnp.arange(n_blocks, dtype=I32)
    block_src = jnp.minimum(block, n_used - 1)
    block_e = jnp.sum((block_src[:, None] * tr >= end_padded[None, :]).astype(I32), axis=1)
    n_e = counts.shape[0]
    ids = jnp.where(counts > 0, jnp.arange(n_e, dtype=I32), n_e)
    later = jnp.concatenate([lax.cummin(ids[::-1])[::-1][1:], jnp.full((1,), n_e, I32)])
    next_e = jnp.where(later < n_e, later, -1).astype(I32)
    return start_padded, block_e.astype(I32), block_src, next_e, n_used.reshape(1)


def kernel(x, c, ada_w, ada_b, w_in, gate_b, pool_w, pool_scale, conv_w, w_branch, w_out, ln1_g, ln1_b,
           router_w, router_bias, exp_w_gate, exp_w_up, exp_w_down, sh_w_gate, sh_w_up, sh_w_down, ln2_g, ln2_b):
    b, seq, d = x.shape
    depth = ada_w.shape[0]
    n = b * seq
    alpha = (2 * depth) ** 0.25
    expert_rows = 256
    n_blocks = -(-(n * TOP_K + N_EXPERTS * (expert_rows - 1)) // expert_rows)

    w = BRANCH_WIDTH
    cb = w // LANES
    sq, sk, sv = 0, cb, 2 * cb
    pool_col = 3 * w
    mq_col = 4 * w
    mq, mk, mv = mq_col // LANES, mq_col // LANES + cb, mq_col // LANES + 2 * cb
    conv_col = 7 * w
    gate_col = 10 * w

    mod = _ada(c, ada_w, ada_b)
    cos_t, sin_t = _rope_tables(seq)
    x2 = x.reshape(n, d)
    u = _modulate(x2, mod[0], seq)
    for l in range(depth):
        proj = _inproj(u, w_in, l, cos_t, sin_t, seq, rot_col=mq_col)
        proj3 = proj.reshape(b, seq, -1)
        y_sb = _sb_attention(proj3, sq, sk, sv).reshape(n, w)
        y_mb = _moba_attention(proj3, mq, mk, mv).reshape(n, w)
        y_pool, y_cv = _local_mixers(proj, pool_w[l], pool_scale[l], conv_w[l], seq,
                                     pool_col // w, conv_col // w, conv_col // w + 1, conv_col // w + 2)
        merged = _merge((y_sb, y_pool, y_mb, y_cv), proj, w_branch, gate_b[l], l, gate_col)
        x2, u2, logits = _outproj_norm(merged, w_out, l, x2, mod[l], ln1_g, ln1_b, router_w, alpha, seq)
        idx_t, w_t, pos_t, counts = _route(logits, router_bias, l)
        counts = counts[:, 0]
        start_padded, block_e, block_src, next_e, n_used = _dispatch_plan(counts, expert_rows, n_blocks)
        xs, dest_t = _dispatch(u2, idx_t, pos_t, counts, start_padded, n_used, expert_rows, n_blocks)
        y_sorted = _experts(xs, block_e, block_src, next_e, n_used, exp_w_gate, exp_w_up, exp_w_down, l,
                            expert_rows)
        shared = _shared_expert(u2, sh_w_gate, sh_w_up, sh_w_down, l)
        x2, u = _combine_norm(dest_t, w_t, y_sorted, shared, x2, mod[l], mod[(l + 1) % depth],
                              ln2_g, ln2_b, l, alpha, seq)
    return x2.reshape(b, seq, d)
```

```python
import functools

import jax
import jax.numpy as jnp
from jax import lax
from jax.experimental import pallas as pl
from jax.experimental.pallas import tpu as pltpu

F32 = jnp.float32
BF16 = jnp.bfloat16
I32 = jnp.int32

HEAD_DIM = 64
N_BRANCH = 4
BRANCH_WIDTH = 512
POOL_WINDOWS = (2, 4, 8, 16)
POOL_GROUP = 128
POOL_HALO = 16
CONV_K = 3
MOBA_BLOCK = 256
MOBA_TOPK = 3
ROPE_THETA = 10000.0
N_EXPERTS = 64
TOP_K = 8
EXPERT_HIDDEN = 384
ROUTED_SCALE = 2.5
LN_EPS = 1e-5
NEG_INF = float("-inf")
SB_DEAD_LOG = -100.0
MASKED = -1e30

LANES = 128
VMEM_LIMIT = 56 * 1024 * 1024


def _cparams(n_axes, vmem=VMEM_LIMIT):
    return pltpu.CompilerParams(dimension_semantics=("arbitrary",) * n_axes, vmem_limit_bytes=vmem)


def _dot(a, b):
    return jnp.dot(a, b, preferred_element_type=F32)


def _dot_nt(a, b):
    return lax.dot_general(a, b, (((1,), (1,)), ((), ())), preferred_element_type=F32)


def _split_bf16(x):
    hi = x.astype(BF16)
    lo = (x - hi.astype(F32)).astype(BF16)
    return hi, lo


def _ada_kernel(c_ref, w_ref, b_ref, o_ref):
    c = c_ref[...]
    cond = c * jax.nn.sigmoid(c)
    o_ref[...] = jnp.dot(cond, w_ref[...], preferred_element_type=F32,
                         precision=lax.Precision.HIGHEST) + b_ref[...]


def _ada(c, ada_w, ada_b):
    depth, d, d6 = ada_w.shape
    b = c.shape[0]
    rows = 8
    tn = 1024
    c8 = jnp.zeros((rows, d), F32).at[:b].set(c)
    out = pl.pallas_call(
        _ada_kernel,
        out_shape=jax.ShapeDtypeStruct((depth, rows, d6), F32),
        grid=(depth, d6 // tn),
        in_specs=[pl.BlockSpec((rows, d), lambda l, n: (0, 0)),
                  pl.BlockSpec((None, d, tn), lambda l, n: (l, 0, n)),
                  pl.BlockSpec((None, 1, tn), lambda l, n: (l, 0, n))],
        out_specs=pl.BlockSpec((None, rows, tn), lambda l, n: (l, 0, n)),
        compiler_params=_cparams(2),
        name="ada_mod",
    )(c8, ada_w, ada_b.reshape(depth, 1, d6))
    return out[:, :b].reshape(depth, b, 6, 1, d)


def _mod_spec(which, d, tiles_per_seq):
    return pl.BlockSpec((None, None, 1, d), lambda i, *_: (i // tiles_per_seq, which, 0, 0))


def _modulate_kernel(x_ref, sh_ref, sc_ref, u_ref):
    u_ref[...] = (x_ref[...] * (1.0 + sc_ref[...]) + sh_ref[...]).astype(u_ref.dtype)


def _modulate(x2, mod_l, seq, tm=512):
    n, d = x2.shape
    tm = min(tm, seq)
    tps = seq // tm
    return pl.pallas_call(
        _modulate_kernel,
        out_shape=jax.ShapeDtypeStruct((n, d), BF16),
        grid=(n // tm,),
        in_specs=[pl.BlockSpec((tm, d), lambda i: (i, 0)), _mod_spec(0, d, tps), _mod_spec(1, d, tps)],
        out_specs=pl.BlockSpec((tm, d), lambda i: (i, 0)),
        compiler_params=_cparams(1),
        name="modulate",
    )(x2, mod_l, mod_l)


def _inproj_kernel(u_ref, w_ref, cos_ref, sin_ref, o_ref, wbf_ref, *, rot_tile, tn):
    n = pl.program_id(0)
    m = pl.program_id(1)

    @pl.when(m == 0)
    def _():
        wbf_ref[...] = w_ref[...].astype(BF16)

    acc = _dot(u_ref[...], wbf_ref[...])

    @pl.when(n == rot_tile)
    def _():
        reps = tn // LANES
        cos = jnp.concatenate([cos_ref[...]] * reps, axis=1)
        sin = jnp.concatenate([sin_ref[...]] * reps, axis=1)
        lane = lax.broadcasted_iota(I32, acc.shape, 1)
        first_half = (lane % HEAD_DIM) < (HEAD_DIM // 2)
        half = HEAD_DIM // 2
        partner = jnp.where(first_half, pltpu.roll(acc, tn - half, 1), pltpu.roll(acc, half, 1))
        o_ref[...] = (acc * cos + partner * sin).astype(o_ref.dtype)

    @pl.when(n != rot_tile)
    def _():
        o_ref[...] = acc.astype(o_ref.dtype)


def _inproj(u, w_in, layer, cos_t, sin_t, seq, rot_col, tm=1024, tn=1024):
    n, d = u.shape
    cols = w_in.shape[2]
    tm = min(tm, seq)
    tps = seq // tm
    assert rot_col % tn == 0
    kern = functools.partial(_inproj_kernel, rot_tile=rot_col // tn, tn=tn)
    return pl.pallas_call(
        kern,
        out_shape=jax.ShapeDtypeStruct((n, cols), BF16),
        grid=(cols // tn, n // tm),
        in_specs=[pl.BlockSpec((tm, d), lambda j, i: (i, 0)),
                  pl.BlockSpec((None, d, tn), lambda j, i: (layer, 0, j)),
                  pl.BlockSpec((tm, LANES), lambda j, i: (i % tps, 0)),
                  pl.BlockSpec((tm, LANES), lambda j, i: (i % tps, 0))],
        out_specs=pl.BlockSpec((tm, tn), lambda j, i: (i, j)),
        scratch_shapes=[pltpu.VMEM((d, tn), BF16)],
        compiler_params=_cparams(2),
        name="in_proj",
    )(u, w_in, cos_t, sin_t)


def _rope_tables(seq):
    half = HEAD_DIM // 2
    inv_freq = ROPE_THETA ** (-jnp.arange(half, dtype=F32) / half)
    ang = jnp.arange(seq, dtype=F32)[:, None] * inv_freq[None, :]
    cos, sin = jnp.cos(ang), jnp.sin(ang)
    cos_h = jnp.concatenate([cos, cos], axis=1)
    sin_h = jnp.concatenate([-sin, sin], axis=1)
    reps = LANES // HEAD_DIM
    return jnp.tile(cos_h, (1, reps)), jnp.tile(sin_h, (1, reps))


def _sb_kernel(q_ref, k_ref, v_ref, o_ref, *, tq, tk):
    qi = pl.program_id(2)
    q0 = qi * tq
    r = lax.broadcasted_iota(I32, (tk, tk), 0)
    c = lax.broadcasted_iota(I32, (tk, tk), 1)
    later = (r > c).astype(BF16)
    heads = LANES // HEAD_DIM
    qs = [q_ref[:, h * HEAD_DIM:(h + 1) * HEAD_DIM] * (HEAD_DIM ** -0.5) for h in range(heads)]

    def block(j0, h, carry, acc, masked):
        k = k_ref[pl.ds(j0, tk), h * HEAD_DIM:(h + 1) * HEAD_DIM]
        v = v_ref[pl.ds(j0, tk), h * HEAD_DIM:(h + 1) * HEAD_DIM]
        z = _dot_nt(qs[h], k)
        softplus = jnp.maximum(z, 0.0) + jnp.log(1.0 + jnp.exp(-jnp.abs(z)))
        log_keep = -softplus
        if masked:
            qpos = q0 + lax.broadcasted_iota(I32, (tq, tk), 0)
            kpos = j0 + lax.broadcasted_iota(I32, (tq, tk), 1)
            past = kpos < qpos
            log_keep = jnp.where(past, log_keep, 0.0)
        hi, lo = _split_bf16(log_keep)
        between = _dot(hi, later) + _dot(lo, later) + carry
        w = jnp.exp((z - softplus) + between)
        if masked:
            w = jnp.where(past, w, 0.0)
        acc = acc + _dot(w.astype(BF16), v)
        carry = carry + jnp.sum(log_keep, axis=1, keepdims=True)
        return carry, acc

    state = [(jnp.zeros((tq, 1), F32), jnp.zeros((tq, HEAD_DIM), F32)) for _ in range(heads)]
    n_diag = tq // tk
    for d in range(n_diag - 1, -1, -1):
        j0 = pl.multiple_of(q0 + d * tk, tk)
        state = [block(j0, h, *state[h], True) for h in range(heads)]

    def live(pairs):
        return functools.reduce(jnp.maximum, [jnp.max(c) for c, _ in pairs])

    def cond(st):
        return jnp.logical_and(st[0] < qi, st[1] > SB_DEAD_LOG)

    def body(st):
        jj = st[0]
        new = [(st[2 + 2 * h], st[3 + 2 * h]) for h in range(heads)]
        for part in range(n_diag):
            j0 = pl.multiple_of(q0 - (jj * n_diag + part + 1) * tk, tk)
            new = [block(j0, h, *new[h], False) for h in range(heads)]
        return (jj + 1, live(new)) + tuple(x for pair in new for x in pair)

    final = lax.while_loop(cond, body, (jnp.int32(0), live(state)) + tuple(x for pair in state for x in pair))
    o_ref[...] = jnp.concatenate([final[3 + 2 * h] for h in range(heads)], axis=1).astype(o_ref.dtype)


def _sb_attention(proj3, q_blk, k_blk, v_blk, tq=256, tk=128):
    b, seq, _ = proj3.shape
    pairs = BRANCH_WIDTH // LANES
    kern = functools.partial(_sb_kernel, tq=tq, tk=tk)
    return pl.pallas_call(
        kern,
        out_shape=jax.ShapeDtypeStruct((b, seq, BRANCH_WIDTH), BF16),
        grid=(b, pairs, seq // tq),
        in_specs=[pl.BlockSpec((None, tq, LANES), lambda bi, p, qi: (bi, qi, q_blk + p)),
                  pl.BlockSpec((None, seq, LANES), lambda bi, p, qi: (bi, 0, k_blk + p)),
                  pl.BlockSpec((None, seq, LANES), lambda bi, p, qi: (bi, 0, v_blk + p))],
        out_specs=pl.BlockSpec((None, tq, LANES), lambda bi, p, qi: (bi, qi, p)),
        compiler_params=_cparams(3),
        name="sb_attention",
    )(proj3, proj3, proj3)


def _moba_kernel(q_ref, k_ref, v_ref, o_ref, kmean_ref, kt_ref, vaug_ref, s_ref, *, n_kb):
    own = pl.program_id(2)
    tq = MOBA_BLOCK
    heads = LANES // HEAD_DIM
    last = n_kb - 1

    @pl.when(own == 0)
    def _():
        s_ref[...] = jnp.zeros_like(s_ref)
        ones = jnp.ones((MOBA_BLOCK, HEAD_DIM), BF16)
        for jb in range(n_kb):
            rows = slice(jb * MOBA_BLOCK, (jb + 1) * MOBA_BLOCK)
            kb = k_ref[rows, :]
            kmean_ref[jb:jb + 1, :] = jnp.mean(kb.astype(F32), axis=0, keepdims=True)
            kt_ref[jb] = kb.T
            vb = v_ref[rows, :]
            for h in range(heads):
                vaug_ref[h, rows, :] = jnp.concatenate([vb[:, h * HEAD_DIM:(h + 1) * HEAD_DIM], ones], axis=1)

    blk = lax.broadcasted_iota(I32, (n_kb, tq), 0)
    row = lax.broadcasted_iota(I32, (tq, MOBA_BLOCK), 0)
    col = lax.broadcasted_iota(I32, (tq, MOBA_BLOCK), 1)
    causal = col <= row
    block_id = lax.broadcasted_iota(I32, (n_kb, MOBA_BLOCK), 0)

    qs, sel_bias, run_max = [], [], []
    for h in range(heads):
        hs = slice(h * HEAD_DIM, (h + 1) * HEAD_DIM)
        q = q_ref[:, hs] * (HEAD_DIM ** -0.5)
        qs.append(q)
        km_hi, km_lo = _split_bf16(kmean_ref[:, hs])
        gate = _dot_nt(km_hi, q) + _dot_nt(km_lo, q)
        beaten = jnp.zeros((n_kb, tq), I32)
        for jp in range(n_kb):
            g = gate[jp:jp + 1, :]
            beats = (g > gate) | ((g == gate) & (jp < blk))
            beaten = beaten + jnp.where(beats, 1, 0) * (jp < own).astype(I32)
        chosen = (beaten < MOBA_TOPK) & (blk < own)
        sel_bias.append(jnp.where(chosen, 0.0, MASKED).T.astype(BF16))
        run_max.append(jnp.full((tq, MOBA_BLOCK), NEG_INF, F32))

    def score_pass(t, running):
        out = list(running)
        for half in range(2):
            j = 2 * t + half
            valid = j < own
            jc = jnp.minimum(j, last)
            pick = (block_id == jc).astype(BF16)
            for h in range(heads):
                hs = slice(h * HEAD_DIM, (h + 1) * HEAD_DIM)
                s = _dot(qs[h], kt_ref[jc, hs, :]) + _dot(sel_bias[h], pick)
                s_ref[h, jnp.where(valid, j, n_kb)] = s
                out[h] = jnp.where(valid, jnp.maximum(out[h], s), out[h])
        return tuple(out)

    run_max = lax.fori_loop(0, (own + 1) // 2, score_pass, tuple(run_max))
    row_max = []
    for h in range(heads):
        hs = slice(h * HEAD_DIM, (h + 1) * HEAD_DIM)
        s = jnp.where(causal, _dot(qs[h], kt_ref[own, hs, :]), NEG_INF)
        s_ref[h, own] = s
        row_max.append(jnp.max(jnp.maximum(run_max[h], s), axis=1, keepdims=True))

    def softmax_pass(t, accs):
        out = list(accs)
        for half in range(2):
            j = 2 * t + half
            valid = j <= own
            jc = jnp.minimum(j, last)
            j0 = pl.multiple_of(jc * MOBA_BLOCK, MOBA_BLOCK)
            for h in range(heads):
                p = jnp.where(valid, jnp.exp(s_ref[h, jc] - row_max[h]), 0.0)
                out[h] = out[h] + _dot(p.astype(BF16), vaug_ref[h, pl.ds(j0, MOBA_BLOCK), :])
        return tuple(out)

    accs = lax.fori_loop(0, (own + 2) // 2, softmax_pass, tuple(jnp.zeros((tq, LANES), F32) for _ in range(heads)))
    o_ref[...] = jnp.concatenate([a[:, :HEAD_DIM] / a[:, HEAD_DIM:] for a in accs], axis=1).astype(o_ref.dtype)


def _moba_attention(proj3, q_blk, k_blk, v_blk):
    b, seq, _ = proj3.shape
    assert seq % MOBA_BLOCK == 0
    n_kb = seq // MOBA_BLOCK
    pairs = BRANCH_WIDTH // LANES
    kern = functools.partial(_moba_kernel, n_kb=n_kb)
    return pl.pallas_call(
        kern,
        out_shape=jax.ShapeDtypeStruct((b, seq, BRANCH_WIDTH), BF16),
        grid=(b, pairs, n_kb),
        in_specs=[pl.BlockSpec((None, MOBA_BLOCK, LANES), lambda bi, p, qi: (bi, qi, q_blk + p)),
                  pl.BlockSpec((None, seq, LANES), lambda bi, p, qi: (bi, 0, k_blk + p)),
                  pl.BlockSpec((None, seq, LANES), lambda bi, p, qi: (bi, 0, v_blk + p))],
        out_specs=pl.BlockSpec((None, MOBA_BLOCK, LANES), lambda bi, p, qi: (bi, qi, p)),
        scratch_shapes=[pltpu.VMEM((n_kb, LANES), F32),
                        pltpu.VMEM((n_kb, LANES, MOBA_BLOCK), BF16),
                        pltpu.VMEM((LANES // HEAD_DIM, seq, LANES), BF16),
                        pltpu.VMEM((LANES // HEAD_DIM, n_kb + 1, MOBA_BLOCK, MOBA_BLOCK), F32)],
        compiler_params=_cparams(3),
        name="moba_attention",
    )(proj3, proj3, proj3)


def _local_kernel(p_ref, ph_ref, ch_ref, chh_ref, cb_ref, cc_ref, cch_ref, pw_ref, ps_ref, cw_ref,
                  yp_ref, yc_ref, *, tm, tiles_per_seq):
    i = pl.program_id(0)
    seq_tile = i % tiles_per_seq
    has_left = seq_tile > 0

    def with_halo(halo_ref, body_ref):
        halo = jnp.where(has_left, halo_ref[...].astype(F32), 0.0)
        return jnp.concatenate([halo, body_ref[...].astype(F32)], axis=0)

    x = with_halo(ph_ref, p_ref)
    pos = seq_tile * tm + lax.broadcasted_iota(I32, (tm, 1), 0)
    for g, win in enumerate(POOL_WINDOWS):
        gs = slice(g * POOL_GROUP, (g + 1) * POOL_GROUP)
        xg = x[:, gs]
        s, k = xg, 1
        while k < win:
            s = s[k:] + s[:-k]
            k *= 2
        start = POOL_HALO + 1 - win
        window_sum = s[start:start + tm]
        count = jnp.minimum(pos + 1, win).astype(F32)
        mixed = window_sum / count - xg[POOL_HALO:]
        y = _dot(mixed.astype(BF16), pw_ref[g].astype(BF16)) * ps_ref[:, gs]
        yp_ref[:, gs] = y.astype(yp_ref.dtype)

    z = with_halo(chh_ref, ch_ref) * with_halo(cch_ref, cc_ref)
    y = jnp.zeros((tm, z.shape[1]), F32)
    for k in range(CONV_K):
        off = POOL_HALO - (CONV_K - 1) + k
        y = y + cw_ref[k:k + 1, :] * z[off:off + tm]
    yc_ref[...] = (cb_ref[...].astype(F32) * y).astype(yc_ref.dtype)


def _local_mixers(proj, pool_w, pool_scale, conv_w, seq, pool_blk, ch_blk, cb_blk, cc_blk, tm=512):
    n = proj.shape[0]
    w = BRANCH_WIDTH
    tm = min(tm, seq)
    tps = seq // tm
    hpt = tm // POOL_HALO

    def body(blk):
        return pl.BlockSpec((tm, w), lambda i: (i, blk))

    def halo(blk):
        return pl.BlockSpec((POOL_HALO, w), lambda i: (jnp.maximum(i * hpt - 1, 0), blk))

    kern = functools.partial(_local_kernel, tm=tm, tiles_per_seq=tps)
    return pl.pallas_call(
        kern,
        out_shape=(jax.ShapeDtypeStruct((n, w), BF16), jax.ShapeDtypeStruct((n, w), BF16)),
        grid=(n // tm,),
        in_specs=[body(pool_blk), halo(pool_blk), body(ch_blk), halo(ch_blk), body(cb_blk),
                  body(cc_blk), halo(cc_blk),
                  pl.BlockSpec(pool_w.shape, lambda i: (0, 0, 0)),
                  pl.BlockSpec((1, w), lambda i: (0, 0)),
                  pl.BlockSpec((CONV_K, w), lambda i: (0, 0))],
        out_specs=(pl.BlockSpec((tm, w), lambda i: (i, 0)), pl.BlockSpec((tm, w), lambda i: (i, 0))),
        compiler_params=_cparams(1),
        name="pool_conv",
    )(proj, proj, proj, proj, proj, proj, proj, pool_w, pool_scale.reshape(1, w), conv_w)


def _merge_kernel(b0, b1, b2, b3, g0, g1, g2, g3, wb_ref, gb_ref, o_ref, wbf_ref):
    @pl.when(pl.program_id(1) == 0)
    def _():
        wbf_ref[...] = wb_ref[...].astype(BF16)

    merged = None
    for nb, (br, gr) in enumerate(((b0, g0), (b1, g1), (b2, g2), (b3, g3))):
        lifted = _dot(br[...], wbf_ref[nb])
        gate = jax.nn.sigmoid(gr[...].astype(F32) + gb_ref[nb])
        term = gate * lifted
        merged = term if merged is None else merged + term
    o_ref[...] = merged.astype(o_ref.dtype)


def _merge(branches, proj, w_branch, gate_b, layer, gate_col, tm=1024, tn=512):
    n = proj.shape[0]
    d = w_branch.shape[3]
    assert gate_col % tn == 0 and d % tn == 0
    g_specs = [pl.BlockSpec((tm, tn), functools.partial(lambda j, i, nb: (i, (gate_col + nb * d) // tn + j), nb=nb))
               for nb in range(N_BRANCH)]
    b_specs = [pl.BlockSpec((tm, BRANCH_WIDTH), lambda j, i: (i, 0))] * N_BRANCH
    return pl.pallas_call(
        _merge_kernel,
        out_shape=jax.ShapeDtypeStruct((n, d), BF16),
        grid=(d // tn, n // tm),
        in_specs=b_specs + g_specs + [
            pl.BlockSpec((None, N_BRANCH, BRANCH_WIDTH, tn), lambda j, i: (layer, 0, 0, j)),
            pl.BlockSpec((N_BRANCH, 1, tn), lambda j, i: (0, 0, j))],
        out_specs=pl.BlockSpec((tm, tn), lambda j, i: (i, j)),
        scratch_shapes=[pltpu.VMEM((N_BRANCH, BRANCH_WIDTH, tn), BF16)],
        compiler_params=_cparams(2),
        name="gated_merge",
    )(*branches, proj, proj, proj, proj, w_branch, gate_b.reshape(N_BRANCH, 1, d))


def _post_norm(x, gate, y, g, b, alpha):
    h = alpha * x + (1.0 + gate) * y
    mu = jnp.mean(h, axis=1, keepdims=True)
    hc = h - mu
    var = jnp.mean(hc * hc, axis=1, keepdims=True)
    return hc * lax.rsqrt(var + LN_EPS) * g + b


def _outproj_kernel(mg_ref, w_ref, x_ref, gate_ref, g_ref, b_ref, sh_ref, sc_ref, rw_ref,
                    xo_ref, uo_ref, lg_ref, w_bf, *, alpha):
    @pl.when(pl.program_id(0) == 0)
    def _():
        w_bf[...] = w_ref[...].astype(BF16)

    y = _dot(mg_ref[...], w_bf[...])
    xn = _post_norm(x_ref[...], gate_ref[...], y, g_ref[...], b_ref[...], alpha)
    xo_ref[...] = xn
    u = xn * (1.0 + sc_ref[...]) + sh_ref[...]
    uo_ref[...] = u
    u_hi, u_lo = _split_bf16(u)
    w_hi, w_lo = _split_bf16(rw_ref[...])
    lg_ref[...] = _dot(u_hi, w_hi) + _dot(u_lo, w_hi) + _dot(u_hi, w_lo)


def _outproj_norm(merged, w_out, layer, x2, mod_l, ln_g, ln_b, router_w, alpha, seq, tm=256):
    n, d = x2.shape
    e = router_w.shape[2]
    tm = min(tm, seq)
    tps = seq // tm
    kern = functools.partial(_outproj_kernel, alpha=alpha)
    vec = lambda: pl.BlockSpec((None, 1, d), lambda i: (layer, 0, 0))
    row = lambda: pl.BlockSpec((tm, d), lambda i: (i, 0))
    once = pl.Buffered(1)
    return pl.pallas_call(
        kern,
        out_shape=(jax.ShapeDtypeStruct((n, d), F32), jax.ShapeDtypeStruct((n, d), F32),
                   jax.ShapeDtypeStruct((n, e), F32)),
        grid=(n // tm,),
        in_specs=[row(),
                  pl.BlockSpec((None, d, d), lambda i: (layer, 0, 0), pipeline_mode=once),
                  row(),
                  _mod_spec(2, d, tps), vec(), vec(), _mod_spec(3, d, tps), _mod_spec(4, d, tps),
                  pl.BlockSpec((None, d, e), lambda i: (layer, 0, 0), pipeline_mode=once)],
        out_specs=(row(), row(), pl.BlockSpec((tm, e), lambda i: (i, 0))),
        scratch_shapes=[pltpu.VMEM((d, d), BF16)],
        compiler_params=_cparams(1),
        name="out_proj_norm",
    )(merged, w_out, x2, mod_l, ln_g.reshape(-1, 1, d), ln_b.reshape(-1, 1, d), mod_l, mod_l, router_w)


def _route_kernel(lg_ref, bias_ref, idx_ref, w_ref, pos_ref, cnt_ref, carry_ref, *, tm):
    i = pl.program_id(0)

    @pl.when(i == 0)
    def _():
        carry_ref[...] = jnp.zeros_like(carry_ref)

    aff = jax.nn.sigmoid(lg_ref[...].T)
    e = aff.shape[0]
    sub = lax.broadcasted_iota(I32, (e, tm), 0).astype(F32)
    work = aff + bias_ref[...]
    picks = []
    chosen_f = jnp.zeros((e, tm), F32)
    for _ in range(TOP_K):
        best = jnp.max(work, axis=0, keepdims=True)
        pick = jnp.min(jnp.where(work == best, sub, float(e)), axis=0, keepdims=True)
        onehot = sub == pick
        picks.append((pick, onehot))
        chosen_f = jnp.where(onehot, 1.0, chosen_f)
        work = jnp.where(onehot, NEG_INF, work)

    sel_aff = chosen_f * aff
    dense_w = sel_aff / jnp.sum(sel_aff, axis=0, keepdims=True) * ROUTED_SCALE
    r = lax.broadcasted_iota(I32, (tm, tm), 0)
    c = lax.broadcasted_iota(I32, (tm, tm), 1)
    earlier = (r < c).astype(BF16)
    rank = _dot(chosen_f.astype(BF16), earlier) + carry_ref[...]
    carry_ref[...] += jnp.sum(chosen_f, axis=1, keepdims=True)
    cnt_ref[...] = carry_ref[...].astype(I32)

    slot = lax.broadcasted_iota(I32, (TOP_K, tm), 0)
    idx8 = jnp.zeros((TOP_K, tm), F32)
    w8 = jnp.zeros((TOP_K, tm), F32)
    pos8 = jnp.zeros((TOP_K, tm), F32)
    for kk, (pick, onehot) in enumerate(picks):
        idx8 = jnp.where(slot == kk, pick, idx8)
        w8 = jnp.where(slot == kk, jnp.sum(jnp.where(onehot, dense_w, 0.0), axis=0, keepdims=True), w8)
        pos8 = jnp.where(slot == kk, jnp.sum(jnp.where(onehot, rank, 0.0), axis=0, keepdims=True), pos8)
    idx_ref[...] = idx8.astype(I32)
    w_ref[...] = w8
    pos_ref[...] = pos8.astype(I32)


def _route(logits, router_bias, layer, tm=512):
    n, e = logits.shape
    kern = functools.partial(_route_kernel, tm=tm)
    slots = lambda: pl.BlockSpec((TOP_K, tm), lambda i: (0, i))
    return pl.pallas_call(
        kern,
        out_shape=(jax.ShapeDtypeStruct((TOP_K, n), I32), jax.ShapeDtypeStruct((TOP_K, n), F32),
                   jax.ShapeDtypeStruct((TOP_K, n), I32), jax.ShapeDtypeStruct((e, 1), I32)),
        grid=(n // tm,),
        in_specs=[pl.BlockSpec((tm, e), lambda i: (i, 0)),
                  pl.BlockSpec((None, e, 1), lambda i: (layer, 0, 0))],
        out_specs=(slots(), slots(), slots(), pl.BlockSpec((e, 1), lambda i: (0, 0))),
        scratch_shapes=[pltpu.VMEM((e, 1), F32)],
        compiler_params=_cparams(1),
        name="route",
    )(logits, router_bias.reshape(-1, e, 1))


def _dispatch_kernel(cnt_ref, start_ref, nu_ref, u_ref, idx_ref, pos_ref, dest_ref, xs_hbm,
                     slab_ref, dest_smem, zrow_ref, zblk_ref, idx_sem, row_sem, zero_sem,
                     *, tm, tr, n_tiles, n_blocks, n_experts):
    i = pl.program_id(0)
    slab_ref[...] = u_ref[...].astype(slab_ref.dtype).reshape(slab_ref.shape)
    idx = idx_ref[...]
    dest = pos_ref[...]
    for e in range(n_experts):
        dest = dest + jnp.where(idx == e, start_ref[e], 0)
    dest_ref[...] = dest
    idx_cp = pltpu.make_async_copy(dest_ref, dest_smem, idx_sem)
    idx_cp.start()
    idx_cp.wait()

    def issue(g, carry):
        base = pl.multiple_of(g * 8, 8)
        for u in range(8):
            for kk in range(TOP_K):
                dst = dest_smem[kk, base + u]
                pltpu.make_async_copy(slab_ref.at[base + u], xs_hbm.at[dst], row_sem).start(priority=kk % 2)
        return carry

    lax.fori_loop(0, tm // 8, issue, 0)
    for kk in range(TOP_K):
        pltpu.make_async_copy(slab_ref, xs_hbm.at[pl.ds(0, tm)], row_sem).wait()

    @pl.when(i == n_tiles - 1)
    def _():
        zrow_ref[...] = jnp.zeros_like(zrow_ref)
        zblk_ref[...] = jnp.zeros_like(zblk_ref)

        def pad_expert(e, carry):
            cnt = cnt_ref[e]
            first = start_ref[e] + cnt
            n_pad = (tr - cnt % tr) % tr

            def start(r, c):
                pltpu.make_async_copy(zrow_ref, xs_hbm.at[first + r], zero_sem).start()
                return c

            def wait(r, c):
                pltpu.make_async_copy(zrow_ref, xs_hbm.at[0], zero_sem).wait()
                return c

            lax.fori_loop(0, n_pad, start, 0)
            lax.fori_loop(0, n_pad, wait, 0)
            return carry

        lax.fori_loop(0, n_experts, pad_expert, 0)

        def tail_start(blk, c):
            row0 = pl.multiple_of(blk * tr, tr)
            pltpu.make_async_copy(zblk_ref, xs_hbm.at[pl.ds(row0, tr)], zero_sem).start()
            return c

        def tail_wait(blk, c):
            pltpu.make_async_copy(zblk_ref, xs_hbm.at[pl.ds(0, tr)], zero_sem).wait()
            return c

        lax.fori_loop(nu_ref[0], n_blocks, tail_start, 0)
        lax.fori_loop(nu_ref[0], n_blocks, tail_wait, 0)


def _dispatch(u2, idx_t, pos_t, counts, start_padded, n_used, tr, n_blocks, tm=1024):
    n, d = u2.shape
    n_tiles = n // tm
    slab = (d // LANES, LANES)
    kern = functools.partial(_dispatch_kernel, tm=tm, tr=tr, n_tiles=n_tiles, n_blocks=n_blocks,
                             n_experts=counts.shape[0])
    slots = lambda: pl.BlockSpec((TOP_K, tm), lambda i, *_: (0, i))
    grid_spec = pltpu.PrefetchScalarGridSpec(
        num_scalar_prefetch=3,
        grid=(n_tiles,),
        in_specs=[pl.BlockSpec((tm, d), lambda i, *_: (i, 0)), slots(), slots()],
        out_specs=(slots(), pl.BlockSpec(memory_space=pl.ANY)),
        scratch_shapes=[pltpu.VMEM((tm,) + slab, BF16), pltpu.SMEM((TOP_K, tm), I32),
                        pltpu.VMEM(slab, BF16), pltpu.VMEM((tr,) + slab, BF16),
                        pltpu.SemaphoreType.DMA(()), pltpu.SemaphoreType.DMA(()), pltpu.SemaphoreType.DMA(())])
    dest_t, xs = pl.pallas_call(
        kern,
        out_shape=(jax.ShapeDtypeStruct((TOP_K, n), I32), jax.ShapeDtypeStruct((n_blocks * tr,) + slab, BF16)),
        grid_spec=grid_spec,
        compiler_params=_cparams(1),
        name="dispatch",
    )(counts, start_padded, n_used, u2, idx_t, pos_t)
    return xs, dest_t


def _experts_kernel(be_ref, bs_ref, nxt_ref, nu_ref, x_ref, wg_hbm, wu_hbm, wd_hbm, y_ref,
                    wg_buf, wu_buf, wd_buf, wgu_bf, wd_bf, h_buf, state_ref, sems, *, hid, layer, n_blocks):
    i = pl.program_id(0)
    n_used = nu_ref[0]
    blk = jnp.minimum(i, n_blocks - 1)

    def fetch(e, s):
        return (pltpu.make_async_copy(wg_hbm.at[layer, e], wg_buf.at[s], sems.at[s, 0]),
                pltpu.make_async_copy(wu_hbm.at[layer, e], wu_buf.at[s], sems.at[s, 1]),
                pltpu.make_async_copy(wd_hbm.at[layer, e], wd_buf.at[s], sems.at[s, 2]))

    @pl.when(i == 0)
    def _():
        state_ref[0] = 0
        state_ref[1] = 0
        for cp in fetch(be_ref[0], 0):
            cp.start()

    down_slot = state_ref[1]
    e = be_ref[blk]
    changed = jnp.logical_and(i < n_used, jnp.logical_or(i == 0, e != be_ref[jnp.maximum(blk - 1, 0)]))

    @pl.when(changed)
    def _():
        s = state_ref[0]
        for cp in fetch(e, s):
            cp.wait()
        nxt = nxt_ref[e]

        @pl.when(nxt >= 0)
        def _():
            for cp in fetch(nxt, 1 - s):
                cp.start()

        wgu_bf[:, :hid] = wg_buf[s].astype(BF16)
        wgu_bf[:, hid:] = wu_buf[s].astype(BF16)
        wd_bf[1 - down_slot] = wd_buf[s].astype(BF16)
        state_ref[0] = 1 - s
        state_ref[1] = 1 - down_slot

    def gate_up():
        x = x_ref[...].reshape(x_ref.shape[0], -1)
        gu = _dot(x, wgu_bf[...])
        g = gu[:, :hid]
        h_buf[i % 2] = ((g * jax.nn.sigmoid(g)) * gu[:, hid:]).astype(BF16)

    def down():
        y_ref[...] = _dot(h_buf[(i + 1) % 2], wd_bf[down_slot]).astype(y_ref.dtype).reshape(y_ref.shape)

    @pl.when(jnp.logical_and(i >= 1, i < n_used))
    def _():
        down()
        gate_up()

    @pl.when(i == 0)
    def _():
        gate_up()

    @pl.when(i == n_used)
    def _():
        down()

    @pl.when(i > n_used)
    def _():
        y_ref[...] = jnp.zeros_like(y_ref)


def _experts(xs, block_e, block_src, next_e, n_used, w_gate, w_up, w_down, layer, tr):
    rows = xs.shape[0]
    slab = xs.shape[1:]
    d = slab[0] * slab[1]
    n_blocks = rows // tr
    hid = w_gate.shape[3]
    kern = functools.partial(_experts_kernel, hid=hid, layer=layer, n_blocks=n_blocks)
    hbm = lambda: pl.BlockSpec(memory_space=pl.ANY)
    grid_spec = pltpu.PrefetchScalarGridSpec(
        num_scalar_prefetch=4,
        grid=(n_blocks + 1,),
        in_specs=[pl.BlockSpec((tr,) + slab, lambda i, be, bs, nx, nu: (bs[jnp.minimum(i, n_blocks - 1)], 0, 0)),
                  hbm(), hbm(), hbm()],
        out_specs=pl.BlockSpec((tr,) + slab, lambda i, be, bs, nx, nu: (jnp.maximum(i - 1, 0), 0, 0)),
        scratch_shapes=[pltpu.VMEM((2, d, hid), F32), pltpu.VMEM((2, d, hid), F32), pltpu.VMEM((2, hid, d), F32),
                        pltpu.VMEM((d, 2 * hid), BF16), pltpu.VMEM((2, hid, d), BF16),
                        pltpu.VMEM((2, tr, hid), BF16),
                        pltpu.SMEM((2,), I32), pltpu.SemaphoreType.DMA((2, 3))])
    return pl.pallas_call(
        kern,
        out_shape=jax.ShapeDtypeStruct((rows,) + slab, xs.dtype),
        grid_spec=grid_spec,
        compiler_params=_cparams(1),
        name="routed_experts",
    )(block_e, block_src, next_e, n_used, xs, w_gate, w_up, w_down)


def _shared_kernel(u_ref, wg_ref, wu_ref, wd_ref, o_ref, wg_bf, wu_bf, wd_bf):
    @pl.when(pl.program_id(0) == 0)
    def _():
        wg_bf[...] = wg_ref[...].astype(BF16)
        wu_bf[...] = wu_ref[...].astype(BF16)
        wd_bf[...] = wd_ref[...].astype(BF16)

    xb = u_ref[...].astype(BF16)
    g = _dot(xb, wg_bf[...])
    h = (g * jax.nn.sigmoid(g)) * _dot(xb, wu_bf[...])
    o_ref[...] = _dot(h.astype(BF16), wd_bf[...])


def _shared_expert(u2, sh_gate, sh_up, sh_down, layer, tm=512):
    n, d = u2.shape
    hid = sh_gate.shape[2]
    return pl.pallas_call(
        _shared_kernel,
        out_shape=jax.ShapeDtypeStruct((n, d), F32),
        grid=(n // tm,),
        in_specs=[pl.BlockSpec((tm, d), lambda i: (i, 0)),
                  pl.BlockSpec((None, d, hid), lambda i: (layer, 0, 0)),
                  pl.BlockSpec((None, d, hid), lambda i: (layer, 0, 0)),
                  pl.BlockSpec((None, hid, d), lambda i: (layer, 0, 0))],
        out_specs=pl.BlockSpec((tm, d), lambda i: (i, 0)),
        scratch_shapes=[pltpu.VMEM((d, hid), BF16), pltpu.VMEM((d, hid), BF16), pltpu.VMEM((hid, d), BF16)],
        compiler_params=_cparams(1),
        name="shared_expert",
    )(u2, sh_gate, sh_up, sh_down)


def _combine_kernel(dest_ref, wt_ref, y_hbm, shd_ref, x_ref, gate_ref, g_ref, b_ref, sh_ref, sc_ref,
                    xo_ref, uo_ref, idx_smem, w_smem, ybuf, acc_ref, idx_sem, row_sem, *, tm, n_tiles, alpha):
    i = pl.program_id(0)
    slot = i % 2

    def gather(tile, s):
        copies = (pltpu.make_async_copy(dest_ref.at[tile], idx_smem.at[s], idx_sem.at[0]),
                  pltpu.make_async_copy(wt_ref.at[tile], w_smem.at[s], idx_sem.at[1]))
        for cp in copies:
            cp.start()
        for cp in copies:
            cp.wait()

        def issue(g, carry):
            base = pl.multiple_of(g * 8, 8)
            for u in range(8):
                for kk in range(TOP_K):
                    src = idx_smem[s, kk, base + u]
                    pltpu.make_async_copy(y_hbm.at[src], ybuf.at[s, kk, base + u],
                                          row_sem.at[s]).start(priority=kk % 2)
            return carry

        lax.fori_loop(0, tm // 8, issue, 0)

    @pl.when(i == 0)
    def _():
        gather(0, 0)

    @pl.when(i + 1 < n_tiles)
    def _():
        gather(i + 1, 1 - slot)

    for kk in range(TOP_K):
        pltpu.make_async_copy(y_hbm.at[pl.ds(0, tm)], ybuf.at[slot, kk], row_sem.at[slot]).wait()

    def weigh(t, carry):
        acc = w_smem[slot, 0, t] * ybuf[slot, 0, t].astype(F32)
        for kk in range(1, TOP_K):
            acc = acc + w_smem[slot, kk, t] * ybuf[slot, kk, t].astype(F32)
        acc_ref[t] = acc
        return carry

    lax.fori_loop(0, tm, weigh, 0, unroll=4)
    ffn = shd_ref[...] + acc_ref[...].reshape(shd_ref.shape)
    xn = _post_norm(x_ref[...], gate_ref[...], ffn, g_ref[...], b_ref[...], alpha)
    xo_ref[...] = xn
    uo_ref[...] = (xn * (1.0 + sc_ref[...]) + sh_ref[...]).astype(uo_ref.dtype)


def _combine_norm(dest_t, w_t, y_sorted, shared, x2, mod_l, mod_next, ln_g, ln_b, layer, alpha, seq, tm=256):
    n, d = x2.shape
    slab = y_sorted.shape[1:]
    n_tiles = n // tm
    tm = min(tm, seq)
    tps = seq // tm
    kern = functools.partial(_combine_kernel, tm=tm, n_tiles=n_tiles, alpha=alpha)
    row = lambda: pl.BlockSpec((tm, d), lambda i: (i, 0))
    vec = lambda: pl.BlockSpec((None, 1, d), lambda i: (layer, 0, 0))
    per_tile = lambda a: a.reshape(TOP_K, n_tiles, tm).transpose(1, 0, 2)
    table = lambda: pl.BlockSpec((n_tiles, TOP_K, tm), lambda i: (0, 0, 0))
    return pl.pallas_call(
        kern,
        out_shape=(jax.ShapeDtypeStruct((n, d), F32), jax.ShapeDtypeStruct((n, d), BF16)),
        grid=(n_tiles,),
        in_specs=[table(), table(),
                  pl.BlockSpec(memory_space=pl.ANY),
                  row(), row(), _mod_spec(5, d, tps), vec(), vec(),
                  _mod_spec(0, d, tps), _mod_spec(1, d, tps)],
        out_specs=(row(), row()),
        scratch_shapes=[pltpu.SMEM((2, TOP_K, tm), I32), pltpu.SMEM((2, TOP_K, tm), F32),
                        pltpu.VMEM((2, TOP_K, tm) + slab, y_sorted.dtype), pltpu.VMEM((tm,) + slab, F32),
                        pltpu.SemaphoreType.DMA((2,)), pltpu.SemaphoreType.DMA((2,))],
        compiler_params=_cparams(1),
        name="combine_norm",
    )(per_tile(dest_t), per_tile(w_t), y_sorted, shared, x2, mod_l,
      ln_g.reshape(-1, 1, d), ln_b.reshape(-1, 1, d), mod_next, mod_next)


def _dispatch_plan(counts, tr, n_blocks):
    padded = (counts + tr - 1) // tr * tr
    end_padded = jnp.cumsum(padded)
    start_padded = (end_padded - padded).astype(I32)
    n_used = (end_padded[-1] // tr).astype(I32)
    block = jnp.arange(n_blocks, dtype=I32)
    block_src = jnp.minimum(block, n_used - 1)
    block_e = jnp.sum((block_src[:, None] * tr >= end_padded[None, :]).astype(I32), axis=1)
    n_e = counts.shape[0]
    ids = jnp.where(counts > 0, jnp.arange(n_e, dtype=I32), n_e)
    later = jnp.concatenate([lax.cummin(ids[::-1])[::-1][1:], jnp.full((1,), n_e, I32)])
    next_e = jnp.where(later < n_e, later, -1).astype(I32)
    return start_padded, block_e.astype(I32), block_src, next_e, n_used.reshape(1)


def kernel(x, c, ada_w, ada_b, w_in, gate_b, pool_w, pool_scale, conv_w, w_branch, w_out, ln1_g, ln1_b,
           router_w, router_bias, exp_w_gate, exp_w_up, exp_w_down, sh_w_gate, sh_w_up, sh_w_down, ln2_g, ln2_b):
    b, seq, d = x.shape
    depth = ada_w.shape[0]
    n = b * seq
    alpha = (2 * depth) ** 0.25
    expert_rows = 256
    n_blocks = -(-(n * TOP_K + N_EXPERTS * (expert_rows - 1)) // expert_rows)

    w = BRANCH_WIDTH
    cb = w // LANES
    sq, sk, sv = 0, cb, 2 * cb
    pool_col = 3 * w
    mq_col = 4 * w
    mq, mk, mv = mq_col // LANES, mq_col // LANES + cb, mq_col // LANES + 2 * cb
    conv_col = 7 * w
    gate_col = 10 * w

    mod = _ada(c, ada_w, ada_b)
    cos_t, sin_t = _rope_tables(seq)
    x2 = x.reshape(n, d)
    u = _modulate(x2, mod[0], seq)
    for l in range(depth):
        proj = _inproj(u, w_in, l, cos_t, sin_t, seq, rot_col=mq_col)
        proj3 = proj.reshape(b, seq, -1)
        y_sb = _sb_attention(proj3, sq, sk, sv).reshape(n, w)
        y_mb = _moba_attention(proj3, mq, mk, mv).reshape(n, w)
        y_pool, y_cv = _local_mixers(proj, pool_w[l], pool_scale[l], conv_w[l], seq,
                                     pool_col // w, conv_col // w, conv_col // w + 1, conv_col // w + 2)
        merged = _merge((y_sb, y_pool, y_mb, y_cv), proj, w_branch, gate_b[l], l, gate_col)
        x2, u2, logits = _outproj_norm(merged, w_out, l, x2, mod[l], ln1_g, ln1_b, router_w, alpha, seq)
        idx_t, w_t, pos_t, counts = _route(logits, router_bias, l)
        counts = counts[:, 0]
        start_padded, block_e, block_src, next_e, n_used = _dispatch_plan(counts, expert_rows, n_blocks)
        xs, dest_t = _dispatch(u2, idx_t, pos_t, counts, start_padded, n_used, expert_rows, n_blocks)
        y_sorted = _experts(xs, block_e, block_src, next_e, n_used, exp_w_gate, exp_w_up, exp_w_down, l,
                            expert_rows)
        shared = _shared_expert(u2, sh_w_gate, sh_w_up, sh_w_down, l)
        x2, u = _combine_norm(dest_t, w_t, y_sorted, shared, x2, mod[l], mod[(l + 1) % depth],
                              ln2_g, ln2_b, l, alpha, seq)
    return x2.reshape(b, seq, d)
```
